```python
import math
import jax, jax.numpy as jnp
from jax import lax
import numpy as np


D_MODEL = 1024
BATCH = 32
SEQ = 2048
DEPTH = 4

D_MIX = D_MODEL
DN_HEADS = 4
DN_HEAD_DIM = 128
DN_WIDTH = DN_HEADS * DN_HEAD_DIM
LRU_WIDTH = D_MIX - DN_WIDTH
LRU_BLOCKS = 8
LRU_BLOCK = LRU_WIDTH // LRU_BLOCKS
LRU_C = 8.0
SHORT_CONV = 4
SHORT_CONV_LEFT = 2
FFN_CONV = 3
FFN_CONV_LEFT = 1
D_FF = 2816
PLE_DIM = 256
CHUNK = 64
EPS = 1e-6

Q_OFF = 0
K_OFF = DN_WIDTH
V_OFF = 2 * DN_WIDTH
Z_OFF = 3 * DN_WIDTH
BETA_OFF = 4 * DN_WIDTH
ALPHA_OFF = BETA_OFF + 2 * DN_HEADS
LX_OFF = ALPHA_OFF + 2 * DN_HEADS
LG_OFF = LX_OFF + LRU_WIDTH
IN_COLS = LG_OFF + LRU_WIDTH

kernel_name = 'hymba_gdn_rglru_convglu_ple_encoder'


def rmsnorm(x, g):
    x32 = x.astype(jnp.float32)
    y = x32 * lax.rsqrt(jnp.mean(x32 * x32, axis=-1, keepdims=True) + EPS)
    return (y * g.astype(jnp.float32)).astype(x.dtype)


def l2norm(x):
    x32 = x.astype(jnp.float32)
    return (x32 * lax.rsqrt(jnp.sum(x32 * x32, axis=-1, keepdims=True) + EPS)).astype(x.dtype)


def dwconv(x, w, left):
    k = w.shape[0]
    s = x.shape[1]
    xp = jnp.pad(x, ((0, 0), (left, k - 1 - left), (0, 0)))
    out = xp[:, 0:s] * w[0]
    for j in range(1, k):
        out = out + xp[:, j:j + s] * w[j]
    return out


def flip(t):
    return jnp.flip(t, axis=1)


def gated_delta_chunked(q, k, v, g, beta):
    B, S, H, Dk = q.shape
    Dv = v.shape[-1]
    N = S // CHUNK
    f32 = jnp.float32

    def chunks(t):
        t = t.astype(f32).reshape((B, N, CHUNK) + t.shape[2:])
        return jnp.moveaxis(t, 3, 1)

    qc = chunks(q) * (Dk ** -0.5)
    kc, vc, gc, bc = chunks(k), chunks(v), chunks(g), chunks(beta)
    gcum = jnp.cumsum(gc, axis=-1)
    idx = jnp.arange(CHUNK)
    incl = idx[:, None] >= idx[None, :]
    strict = idx[:, None] > idx[None, :]
    decay = jnp.exp(jnp.where(incl, gcum[..., :, None] - gcum[..., None, :], -jnp.inf))
    kb = kc * bc[..., None]
    a_mat = jnp.where(strict, jnp.einsum('bhnid,bhnjd->bhnij', kb, kc) * decay, 0.0)
    rhs = jnp.concatenate([vc * bc[..., None], kb * jnp.exp(gcum)[..., None]], axis=-1)
    sol = lax.linalg.triangular_solve(a_mat, rhs, left_side=True, lower=True, unit_diagonal=True)
    u, w = sol[..., :Dv], sol[..., Dv:]
    attn = jnp.einsum('bhnid,bhnjd->bhnij', qc, kc) * decay
    glast = gcum[..., -1:]
    q_dec = qc * jnp.exp(gcum)[..., None]
    k_dec = kc * jnp.exp(glast - gcum)[..., None]
    cdec = jnp.exp(glast[..., 0])
    xs = tuple(jnp.moveaxis(t, 2, 0) for t in (q_dec, k_dec, w, u, attn, cdec))

    def step(state, inp):
        qd, kd, wi, ui, ai, cd = inp
        v_new = ui - jnp.einsum('bhcd,bhde->bhce', wi, state)
        o = jnp.einsum('bhcd,bhde->bhce', qd, state) + jnp.einsum('bhij,bhje->bhie', ai, v_new)
        state = state * cd[..., None, None] + jnp.einsum('bhcd,bhce->bhde', kd, v_new)
        return state, o

    state0 = jnp.zeros((B, H, Dk, Dv), f32)
    _, o = lax.scan(step, state0, xs)
    o = jnp.transpose(o, (1, 0, 3, 2, 4)).reshape(B, S, H, Dv)
    return o.astype(v.dtype)


def rglru(x, wa, ba, wx, bx, lam):
    B, S, W = x.shape
    xr = x.reshape(B, S, LRU_BLOCKS, LRU_BLOCK)
    r = jax.nn.sigmoid(jnp.einsum('bsnc,ncd->bsnd', xr, wa).reshape(B, S, W) + ba)
    ig = jax.nn.sigmoid(jnp.einsum('bsnc,ncd->bsnd', xr, wx).reshape(B, S, W) + bx)
    log_a = -LRU_C * r.astype(jnp.float32) * jax.nn.softplus(-lam.astype(jnp.float32))
    a = jnp.exp(log_a)
    b = jnp.sqrt(-jnp.expm1(2.0 * log_a)) * (ig * x).astype(jnp.float32)

    def combine(e1, e2):
        a1, b1 = e1
        a2, b2 = e2
        return a1 * a2, a2 * b1 + b2

    _, h = lax.associative_scan(combine, (a, b), axis=1)
    return h.astype(x.dtype)


def setup_inputs(seed: int = 0) -> dict:
    key = jax.random.key(seed)
    ks = iter(jax.random.split(key, 32))
    f32 = jnp.float32

    def nrm(shape, scale):
        return scale * jax.random.normal(next(ks), shape, f32)

    def gain(shape):
        return 1.0 + nrm(shape, 0.02)

    L, H = DEPTH, DN_HEADS
    x = nrm((BATCH, SEQ, D_MODEL), 1.0)
    p = nrm((DEPTH, BATCH, SEQ, PLE_DIM), 1.0)
    norm1_g = gain((L, D_MODEL))
    w_in = nrm((L, D_MODEL, IN_COLS), D_MODEL ** -0.5)
    dn_conv_w = nrm((L, SHORT_CONV, 3 * DN_WIDTH), SHORT_CONV ** -0.5)
    dn_a_log = jnp.log(jax.random.uniform(next(ks), (L, 2, H), f32, 1.0, 16.0))
    dt = jnp.exp(jax.random.uniform(next(ks), (L, 2, H), f32, math.log(1e-3), math.log(1e-1)))
    dn_dt_bias = dt + jnp.log(-jnp.expm1(-dt))
    dn_norm_g = gain((L, DN_HEAD_DIM))
    lru_conv_w = nrm((L, SHORT_CONV, LRU_WIDTH), SHORT_CONV ** -0.5)
    lru_conv_b = nrm((L, LRU_WIDTH), 0.02)
    lru_wa = nrm((L, 2, LRU_BLOCKS, LRU_BLOCK, LRU_BLOCK), LRU_BLOCK ** -0.5)
    lru_ba = nrm((L, 2, LRU_WIDTH), 0.02)
    lru_wx = nrm((L, 2, LRU_BLOCKS, LRU_BLOCK, LRU_BLOCK), LRU_BLOCK ** -0.5)
    lru_bx = nrm((L, 2, LRU_WIDTH), 0.02)
    a0 = jax.random.uniform(next(ks), (L, 2, LRU_WIDTH), f32, 0.9, 0.999) ** (1.0 / LRU_C)
    lru_lambda = jnp.log(a0) - jnp.log1p(-a0)
    lru_norm_g = gain((L, LRU_WIDTH))
    w_out = nrm((L, D_MIX, D_MODEL), D_MIX ** -0.5)
    norm2_g = gain((L, D_MODEL))
    ffn_wg = nrm((L, D_MODEL, D_FF), D_MODEL ** -0.5)
    ffn_wu = nrm((L, D_MODEL, D_FF), D_MODEL ** -0.5)
    ffn_conv_w = nrm((L, FFN_CONV, D_FF), FFN_CONV ** -0.5)
    ffn_conv_b = nrm((L, D_FF), 0.02)
    ffn_wd = nrm((L, D_FF, D_MODEL), D_FF ** -0.5)
    ple_norm_g = gain((L, D_MODEL))
    ple_wg = nrm((L, D_MODEL, D_MODEL), D_MODEL ** -0.5)
    ple_bg = nrm((L, D_MODEL), 0.02)
    ple_wp = nrm((L, PLE_DIM, D_MODEL), PLE_DIM ** -0.5)
    final_g = gain((D_MODEL,))
    return {'x': x, 'p': p, 'norm1_g': norm1_g, 'w_in': w_in, 'dn_conv_w': dn_conv_w,
            'dn_a_log': dn_a_log, 'dn_dt_bias': dn_dt_bias, 'dn_norm_g': dn_norm_g,
            'lru_conv_w': lru_conv_w, 'lru_conv_b': lru_conv_b, 'lru_wa': lru_wa, 'lru_ba': lru_ba,
            'lru_wx': lru_wx, 'lru_bx': lru_bx, 'lru_lambda': lru_lambda, 'lru_norm_g': lru_norm_g,
            'w_out': w_out, 'norm2_g': norm2_g, 'ffn_wg': ffn_wg, 'ffn_wu': ffn_wu,
            'ffn_conv_w': ffn_conv_w, 'ffn_conv_b': ffn_conv_b, 'ffn_wd': ffn_wd,
            'ple_norm_g': ple_norm_g, 'ple_wg': ple_wg, 'ple_bg': ple_bg, 'ple_wp': ple_wp,
            'final_g': final_g}


def reference(x, p, norm1_g, w_in, dn_conv_w, dn_a_log, dn_dt_bias, dn_norm_g,
              lru_conv_w, lru_conv_b, lru_wa, lru_ba, lru_wx, lru_bx, lru_lambda, lru_norm_g,
              w_out, norm2_g, ffn_wg, ffn_wu, ffn_conv_w, ffn_conv_b, ffn_wd,
              ple_norm_g, ple_wg, ple_bg, ple_wp, final_g):
    B, S, _ = x.shape
    H, Dh = DN_HEADS, DN_HEAD_DIM
    r = x
    for i in range(DEPTH):
        h = rmsnorm(r, norm1_g[i])
        proj = h @ w_in[i]
        qkv = jax.nn.silu(dwconv(proj[..., Q_OFF:Z_OFF], dn_conv_w[i], SHORT_CONV_LEFT))
        q = l2norm(qkv[..., Q_OFF:K_OFF].reshape(B, S, H, Dh))
        k = l2norm(qkv[..., K_OFF:V_OFF].reshape(B, S, H, Dh))
        v = qkv[..., V_OFF:Z_OFF].reshape(B, S, H, Dh)
        z = proj[..., Z_OFF:BETA_OFF].reshape(B, S, H, Dh)
        beta = jax.nn.sigmoid(proj[..., BETA_OFF:ALPHA_OFF].reshape(B, S, 2, H))
        alpha = proj[..., ALPHA_OFF:LX_OFF].reshape(B, S, 2, H).astype(jnp.float32)
        g = -jnp.exp(dn_a_log[i].astype(jnp.float32)) * jax.nn.softplus(alpha + dn_dt_bias[i].astype(jnp.float32))
        o_f = gated_delta_chunked(q, k, v, g[:, :, 0], beta[:, :, 0])
        o_b = flip(gated_delta_chunked(flip(q), flip(k), flip(v), flip(g[:, :, 1]), flip(beta[:, :, 1])))
        dn_out = (rmsnorm(o_f + o_b, dn_norm_g[i]) * jax.nn.silu(z)).reshape(B, S, DN_WIDTH)
        xc = dwconv(proj[..., LX_OFF:LG_OFF], lru_conv_w[i], SHORT_CONV_LEFT) + lru_conv_b[i]
        h_f = rglru(xc, lru_wa[i, 0], lru_ba[i, 0], lru_wx[i, 0], lru_bx[i, 0], lru_lambda[i, 0])
        h_b = flip(rglru(flip(xc), lru_wa[i, 1], lru_ba[i, 1], lru_wx[i, 1], lru_bx[i, 1], lru_lambda[i, 1]))
        lru_out = rmsnorm(jax.nn.gelu(proj[..., LG_OFF:IN_COLS]) * (h_f + h_b), lru_norm_g[i])
        r = r + jnp.concatenate([dn_out, lru_out], axis=-1) @ w_out[i]
        h2 = rmsnorm(r, norm2_g[i])
        gate = dwconv(h2 @ ffn_wg[i], ffn_conv_w[i], FFN_CONV_LEFT) + ffn_conv_b[i]
        r = r + (jax.nn.gelu(gate) * (h2 @ ffn_wu[i])) @ ffn_wd[i]
        pg = jax.nn.sigmoid(rmsnorm(r, ple_norm_g[i]) @ ple_wg[i] + ple_bg[i])
        r = r + pg * (p[i] @ ple_wp[i])
    return rmsnorm(r, final_g)
```

```python
import functools

import jax
import jax.numpy as jnp
from jax import lax
from jax.experimental import pallas as pl
from jax.experimental.pallas import tpu as pltpu

D_MODEL = 1024
DN_HEADS = 4
DN_HEAD_DIM = 128
DN_WIDTH = DN_HEADS * DN_HEAD_DIM
LRU_WIDTH = 512
LRU_BLOCKS = 8
LRU_BLOCK = LRU_WIDTH // LRU_BLOCKS
LRU_C = 8.0
D_FF = 2816
PLE_DIM = 256
EPS = 1e-6
Z_OFF = 3 * DN_WIDTH
BETA_OFF = 4 * DN_WIDTH
LX_OFF = BETA_OFF + 4 * DN_HEADS
IN_COLS = LX_OFF + 2 * LRU_WIDTH

LANES = 128
SUBLANES = 8
BF16_ROWS = 16
CHUNK = 128
MAIN_COLS = 4 * DN_WIDTH + 2 * LRU_WIDTH
N_MAIN_BLK = MAIN_COLS // LANES
PROJ_COLS = MAIN_COLS + LANES
TOKEN_TILE = 512
FF_CHUNK = D_FF // 2
VMEM_LIMIT = 56 * 1024 * 1024

MXU_DTYPE = jnp.bfloat16
F32 = jnp.float32


def _mm(a, b):
    return jnp.dot(a.astype(MXU_DTYPE), b.astype(MXU_DTYPE), preferred_element_type=F32)


def _mm_nt(a, b):
    return lax.dot_general(a.astype(MXU_DTYPE), b.astype(MXU_DTYPE), (((1,), (1,)), ((), ())),
                           preferred_element_type=F32)


def _rms(x, g):
    return x * lax.rsqrt(jnp.mean(x * x, axis=-1, keepdims=True) + EPS) * g


def _row_iota(shape):
    return lax.broadcasted_iota(jnp.int32, shape, 0)


def _col_iota(shape):
    return lax.broadcasted_iota(jnp.int32, shape, 1)


def _seq_conv(x, w, left):
    n = x.shape[0]
    row = _row_iota(x.shape)
    out = None
    for j in range(w.shape[0]):
        off = j - left
        if off == 0:
            xs = x
        elif off < 0:
            xs = jnp.where(row >= -off, pltpu.roll(x, -off, 0), 0.0)
        else:
            xs = jnp.where(row < n - off, pltpu.roll(x, n - off, 0), 0.0)
        term = xs * w[j:j + 1, :]
        out = term if out is None else out + term
    return out


def _inproj_kernel(r_ref, g_ref, w_ref, alog_ref, dtb_ref, proj_ref, pack_ref):
    h = _rms(r_ref[...], g_ref[...])
    res = _mm(h, w_ref[...])
    for j in range(N_MAIN_BLK):
        proj_ref[j] = res[:, j * LANES:(j + 1) * LANES]
    gates = res[:, MAIN_COLS:]
    tm = gates.shape[0]
    beta = jax.nn.sigmoid(gates)
    g = -jnp.exp(alog_ref[...]) * jax.nn.softplus(gates + dtb_ref[...])
    sq = (CHUNK, CHUNK)
    tri = jnp.concatenate([(_col_iota(sq) <= _row_iota(sq)).astype(MXU_DTYPE),
                           (_col_iota(sq) >= _row_iota(sq)).astype(MXU_DTYPE)], axis=0)
    lane = _col_iota((CHUNK, LANES))
    for c in range(tm // CHUNK):
        gc = g[c * CHUNK:(c + 1) * CHUNK]
        p1 = gc.astype(MXU_DTYPE)
        r1 = gc - p1.astype(F32)
        p2 = r1.astype(MXU_DTYPE)
        p3 = (r1 - p2.astype(F32)).astype(MXU_DTYPE)
        cs = jnp.dot(tri, jnp.concatenate([p1, p2, p3], axis=1), preferred_element_type=F32)
        cs = cs[:, :LANES] + cs[:, LANES:2 * LANES] + cs[:, 2 * LANES:]
        prefix, suffix = cs[:CHUNK], cs[CHUNK:]
        pk = jnp.where(lane < 8, beta[c * CHUNK:(c + 1) * CHUNK],
                       jnp.where(lane < 12, prefix, suffix))
        for hd in range(DN_HEADS):
            pack_ref[hd, c * CHUNK:(c + 1) * CHUNK, :] = pk if hd == 0 else pltpu.roll(pk, LANES - hd, 1)


def _inproj(r2d, g1, w, alog, dtb):
    t = r2d.shape[0]
    tm = TOKEN_TILE
    return pl.pallas_call(
        _inproj_kernel,
        grid=(t // tm,),
        in_specs=[
            pl.BlockSpec((tm, D_MODEL), lambda i: (i, 0)),
            pl.BlockSpec((1, D_MODEL), lambda i: (0, 0)),
            pl.BlockSpec((D_MODEL, PROJ_COLS), lambda i: (0, 0)),
            pl.BlockSpec((1, LANES), lambda i: (0, 0)),
            pl.BlockSpec((1, LANES), lambda i: (0, 0)),
        ],
        out_specs=[
            pl.BlockSpec((N_MAIN_BLK, tm, LANES), lambda i: (0, i, 0)),
            pl.BlockSpec((DN_HEADS, tm, LANES), lambda i: (0, i, 0)),
        ],
        out_shape=[
            jax.ShapeDtypeStruct((N_MAIN_BLK, t, LANES), F32),
            jax.ShapeDtypeStruct((DN_HEADS, t, LANES), F32),
        ],
        compiler_params=pltpu.CompilerParams(dimension_semantics=("arbitrary",),
                                             vmem_limit_bytes=VMEM_LIMIT),
    )(r2d, g1, w, alog, dtb)


def _unit_tri_inverse_minus_eye(a, upper):
    shape = a.shape
    rb = _row_iota(shape) // SUBLANES
    cb = _col_iota(shape) // SUBLANES
    x = jnp.where(rb == cb, a, 0.0)
    y2 = _mm(x, x)
    y4 = _mm(y2, y2)
    m1 = y2 - x - _mm(x, y2)
    n = m1 + y4 + _mm(m1, y4)
    s = SUBLANES
    while s < shape[0]:
        ra = _row_iota(shape) // s
        ca = _col_iota(shape) // s
        hi, lo = (ca, ra) if upper else (ra, ca)
        blk = ((hi - lo) * 2 + (hi & 1)) == 3
        b = jnp.where(blk, a, 0.0)
        p = b + _mm(n, b)
        n = n - p - _mm(p, n)
        s *= 2
    return n


def _dn_kernel(q_ref, k_ref, v_ref, z_ref, pack_ref, cwq_ref, cwk_ref, cwv_ref, ng_ref, out_ref,
               qs_ref, ks_ref, vs_ref, u_ref, wq_ref, attn_ref, kdt_ref, cd_ref, o_ref):
    seq = q_ref.shape[1]
    nc = seq // CHUNK

    def l2n(x):
        return x * lax.rsqrt(jnp.sum(x * x, axis=-1, keepdims=True) + EPS)

    def conv_silu(ref, cw_ref):
        y = _seq_conv(ref[0], cw_ref[...], 2)
        return y * jax.nn.sigmoid(y)

    qs_ref[...] = l2n(conv_silu(q_ref, cwq_ref)) * (DN_HEAD_DIM ** -0.5)
    ks_ref[...] = l2n(conv_silu(k_ref, cwk_ref))
    vs_ref[...] = conv_silu(v_ref, cwv_ref)
    o_ref[...] = jnp.zeros_like(o_ref)

    sq = (CHUNK, CHUNK)
    ri = _row_iota(sq)
    ci = _col_iota(sq)

    def phase_a(n, carry):
        off = pl.multiple_of(n * CHUNK, CHUNK)
        qc = qs_ref[pl.ds(off, CHUNK), :]
        kc = ks_ref[pl.ds(off, CHUNK), :]
        vc = vs_ref[pl.ds(off, CHUNK), :]
        pk = pack_ref[0, pl.ds(off, CHUNK), :]
        kq = _mm_nt(jnp.concatenate([kc, qc], axis=0), kc)
        kk, qk = kq[:CHUNK], kq[CHUNK:]
        for d in range(2):
            upper = d == 1
            beta = jnp.broadcast_to(pk[:, 4 * d:4 * d + 1], sq)
            gc = jnp.broadcast_to(pk[:, 8 + 4 * d:9 + 4 * d], sq)
            glast = jnp.broadcast_to(gc[0:1, :] if upper else gc[CHUNK - 1:CHUNK, :], sq)
            e_in = jnp.exp(gc)
            e_out = jnp.exp(glast - gc)
            incl = (ri <= ci) if upper else (ri >= ci)
            strict = (ri < ci) if upper else (ri > ci)
            decay = jnp.exp(jnp.where(incl, gc - gc.T, -jnp.inf))
            a = jnp.where(strict, beta * kk * decay, 0.0)
            n_inv = _unit_tri_inverse_minus_eye(a, upper)
            rhs = jnp.concatenate([vc * beta, kc * beta * e_in], axis=1)
            uw = rhs + _mm(n_inv, rhs)
            u_ref[d, pl.ds(off, CHUNK), :] = uw[:, :DN_HEAD_DIM]
            wq_ref[d, n] = jnp.concatenate([uw[:, DN_HEAD_DIM:], qc * e_in], axis=0).astype(wq_ref.dtype)
            attn_ref[d, n] = (qk * decay).astype(attn_ref.dtype)
            kdt_ref[d, n] = (kc * e_out).T.astype(kdt_ref.dtype)
            cd_ref[d, n] = jnp.exp(glast[:SUBLANES, :])
        return carry

    lax.fori_loop(0, nc, phase_a, 0)

    def phase_b(i, states):
        new = []
        for d in range(2):
            n = i if d == 0 else nc - 1 - i
            off = pl.multiple_of(n * CHUNK, CHUNK)
            st = states[d]
            wq = _mm(wq_ref[d, n], st)
            v_new = u_ref[d, pl.ds(off, CHUNK), :] - wq[:CHUNK]
            o = wq[CHUNK:] + _mm(attn_ref[d, n], v_new)
            o_ref[pl.ds(off, CHUNK), :] += o
            cd = cd_ref[d, n][0:1, :]
            new.append(st * cd + _mm(kdt_ref[d, n], v_new))
        return tuple(new)

    zero = jnp.zeros((DN_HEAD_DIM, DN_HEAD_DIM), F32)
    lax.fori_loop(0, nc, phase_b, (zero, zero))

    z = z_ref[0]
    out_ref[0] = _rms(o_ref[...], ng_ref[...]) * (z * jax.nn.sigmoid(z))


def _deltanet(proj, pack, conv_w, norm_g, batch, seq):
    nc = seq // CHUNK
    blk = (1, seq, LANES)

    def col(j0):
        return pl.BlockSpec(blk, lambda b, h: (j0 + h, b, 0))

    def cw(j0):
        return pl.BlockSpec((4, LANES), lambda b, h: (0, j0 + h))

    return pl.pallas_call(
        _dn_kernel,
        grid=(batch, DN_HEADS),
        in_specs=[col(0), col(DN_HEADS), col(2 * DN_HEADS), col(3 * DN_HEADS),
                  pl.BlockSpec(blk, lambda b, h: (h, b, 0)),
                  cw(0), cw(DN_HEADS), cw(2 * DN_HEADS),
                  pl.BlockSpec((1, LANES), lambda b, h: (0, 0))],
        out_specs=pl.BlockSpec(blk, lambda b, h: (h, b, 0)),
        out_shape=jax.ShapeDtypeStruct((DN_HEADS, batch * seq, LANES), F32),
        scratch_shapes=[
            pltpu.VMEM((seq, LANES), F32),
            pltpu.VMEM((seq, LANES), F32),
            pltpu.VMEM((seq, LANES), F32),
            pltpu.VMEM((2, seq, LANES), F32),
            pltpu.VMEM((2, nc, 2 * CHUNK, LANES), MXU_DTYPE),
            pltpu.VMEM((2, nc, CHUNK, CHUNK), MXU_DTYPE),
            pltpu.VMEM((2, nc, LANES, CHUNK), MXU_DTYPE),
            pltpu.VMEM((2, nc, SUBLANES, LANES), F32),
            pltpu.VMEM((seq, LANES), F32),
        ],
        compiler_params=pltpu.CompilerParams(dimension_semantics=("arbitrary", "arbitrary"),
                                             vmem_limit_bytes=VMEM_LIMIT),
    )(proj, proj, proj, proj, pack, conv_w, conv_w, conv_w, norm_g)


def _lru_kernel(lx_ref, lg_ref, cw_ref, cb_ref, wg_ref, bg_ref, lam_ref, y_ref,
                a_ref, b_ref, h_ref):
    seq = lx_ref.shape[1]
    ngrp = seq // SUBLANES
    xc = _seq_conv(lx_ref[0], cw_ref[...], 2) + cb_ref[...]
    gates = _mm(xc, wg_ref[0]) + bg_ref[0]
    sub = _row_iota((seq, LANES)) % SUBLANES
    for d in range(2):
        rev = d == 1
        r = jax.nn.sigmoid(gates[:, (2 * d) * LANES:(2 * d + 1) * LANES])
        ig = jax.nn.sigmoid(gates[:, (2 * d + 1) * LANES:(2 * d + 2) * LANES])
        log_a = -LRU_C * r * jax.nn.softplus(-lam_ref[0, d:d + 1, :])
        a = jnp.exp(log_a)
        b = jnp.sqrt(-jnp.tanh(log_a) * (a * a + 1.0)) * (ig * xc)
        s = 1
        while s < SUBLANES:
            sh = seq - s if rev else s
            keep = (sub < SUBLANES - s) if rev else (sub >= s)
            b = jnp.where(keep, a * pltpu.roll(b, sh, 0) + b, b)
            a = jnp.where(keep, a * pltpu.roll(a, sh, 0), a)
            s *= 2
        a_ref[d] = a
        b_ref[d] = b

    zero = jnp.zeros((1, LANES), F32)

    def fwd_step(i, hprev):
        off = pl.multiple_of(i * SUBLANES, SUBLANES)
        hg = b_ref[0, pl.ds(off, SUBLANES), :] + a_ref[0, pl.ds(off, SUBLANES), :] * hprev
        h_ref[pl.ds(off, SUBLANES), :] = hg
        return hg[SUBLANES - 1:SUBLANES, :]

    def bwd_step(i, hnext):
        off = pl.multiple_of((ngrp - 1 - i) * SUBLANES, SUBLANES)
        hg = b_ref[1, pl.ds(off, SUBLANES), :] + a_ref[1, pl.ds(off, SUBLANES), :] * hnext
        h_ref[pl.ds(off, SUBLANES), :] += hg
        return hg[0:1, :]

    lax.fori_loop(0, ngrp, fwd_step, zero, unroll=8)
    lax.fori_loop(0, ngrp, bwd_step, zero, unroll=8)
    y_ref[0] = jax.nn.gelu(lg_ref[0]) * h_ref[...]


def _rglru(proj, conv_w, conv_b, wgate, bgate, lam, batch, seq):
    nblk = LRU_WIDTH // LANES
    blk = (1, seq, LANES)
    j_lx = 4 * DN_HEADS
    j_lg = j_lx + nblk
    return pl.pallas_call(
        _lru_kernel,
        grid=(batch, nblk),
        in_specs=[
            pl.BlockSpec(blk, lambda b, c: (j_lx + c, b, 0)),
            pl.BlockSpec(blk, lambda b, c: (j_lg + c, b, 0)),
            pl.BlockSpec((4, LANES), lambda b, c: (0, c)),
            pl.BlockSpec((1, LANES), lambda b, c: (0, c)),
            pl.BlockSpec((1, LANES, 4 * LANES), lambda b, c: (c, 0, 0)),
            pl.BlockSpec((1, 1, 4 * LANES), lambda b, c: (c, 0, 0)),
            pl.BlockSpec((1, 2, LANES), lambda b, c: (c, 0, 0)),
        ],
        out_specs=pl.BlockSpec(blk, lambda b, c: (c, b, 0)),
        out_shape=jax.ShapeDtypeStruct((nblk, batch * seq, LANES), F32),
        scratch_shapes=[
            pltpu.VMEM((2, seq, LANES), F32),
            pltpu.VMEM((2, seq, LANES), F32),
            pltpu.VMEM((seq, LANES), F32),
        ],
        compiler_params=pltpu.CompilerParams(dimension_semantics=("arbitrary", "arbitrary"),
                                             vmem_limit_bytes=VMEM_LIMIT),
    )(proj, proj, conv_w, conv_b, wgate, bgate, lam)


def _outproj_kernel(r_ref, dn_ref, ly_ref, lg_ref, w_ref, g2_ref, r1_ref, h2_ref):
    dn = jnp.concatenate([dn_ref[j] for j in range(dn_ref.shape[0])], axis=1)
    ly = jnp.concatenate([ly_ref[j] for j in range(ly_ref.shape[0])], axis=1)
    mix = jnp.concatenate([dn, _rms(ly, lg_ref[...])], axis=1)
    r1 = r_ref[...] + _mm(mix, w_ref[...])
    r1_ref[...] = r1
    h2_ref[...] = _rms(r1, g2_ref[...]).astype(h2_ref.dtype)


def _outproj(r2d, dn, ly, lru_g, w_out, g2):
    t = r2d.shape[0]
    tm = TOKEN_TILE
    nb = LRU_WIDTH // LANES
    return pl.pallas_call(
        _outproj_kernel,
        grid=(t // tm,),
        in_specs=[
            pl.BlockSpec((tm, D_MODEL), lambda i: (i, 0)),
            pl.BlockSpec((DN_HEADS, tm, LANES), lambda i: (0, i, 0)),
            pl.BlockSpec((nb, tm, LANES), lambda i: (0, i, 0)),
            pl.BlockSpec((1, LRU_WIDTH), lambda i: (0, 0)),
            pl.BlockSpec((D_MODEL, D_MODEL), lambda i: (0, 0)),
            pl.BlockSpec((1, D_MODEL), lambda i: (0, 0)),
        ],
        out_specs=[
            pl.BlockSpec((tm, D_MODEL), lambda i: (i, 0)),
            pl.BlockSpec((tm, D_MODEL), lambda i: (i, 0)),
        ],
        out_shape=[
            jax.ShapeDtypeStruct((t, D_MODEL), F32),
            jax.ShapeDtypeStruct((t, D_MODEL), MXU_DTYPE),
        ],
        compiler_params=pltpu.CompilerParams(dimension_semantics=("arbitrary",),
                                             vmem_limit_bytes=VMEM_LIMIT),
    )(r2d, dn, ly, lru_g, w_out, g2)


def _ffn_kernel(hp_ref, hm_ref, hn_ref, r1_ref, p_ref, wg_ref, wu_ref, cw_ref, cb_ref, wd_ref,
                png_ref, pwg_ref, pbg_ref, pwp_ref, out_ref, hcat_ref):
    t = pl.program_id(1)
    tm = hm_ref.shape[1]
    halo = BF16_ROWS
    hcat_ref[0:halo, :] = jnp.where(t > 0, hp_ref[0], jnp.zeros_like(hp_ref[0]))
    hcat_ref[halo:halo + tm, :] = hm_ref[0]
    hcat_ref[halo + tm:, :] = jnp.where(t < pl.num_programs(1) - 1, hn_ref[0], jnp.zeros_like(hn_ref[0]))
    rows = tm + 2 * halo
    acc = r1_ref[0]
    for c in range(D_FF // FF_CHUNK):
        cs = slice(c * FF_CHUNK, (c + 1) * FF_CHUNK)
        gfull = jnp.dot(hcat_ref[...], wg_ref[:, cs], preferred_element_type=F32)
        gate = (pltpu.roll(gfull, 1, 0)[halo:halo + tm] * cw_ref[0:1, cs]
                + gfull[halo:halo + tm] * cw_ref[1:2, cs]
                + pltpu.roll(gfull, rows - 1, 0)[halo:halo + tm] * cw_ref[2:3, cs]
                + cb_ref[:, cs])
        up = jnp.dot(hm_ref[0], wu_ref[:, cs], preferred_element_type=F32)
        acc = acc + _mm(jax.nn.gelu(gate) * up, wd_ref[cs, :])
    pgate = jax.nn.sigmoid(_mm(_rms(acc, png_ref[...]), pwg_ref[...]) + pbg_ref[...])
    out_ref[0] = acc + pgate * _mm(p_ref[0, 0], pwp_ref[...])


def _ffn(h2, r1, p, layer, wg, wu, cw, cb, wd, png, pwg, pbg, pwp):
    batch, seq, _ = h2.shape
    tm = TOKEN_TILE
    nt = seq // tm
    hb = tm // BF16_ROWS
    last_halo = seq // BF16_ROWS - 1

    def whole(shape):
        return pl.BlockSpec(shape, lambda b, t: (0,) * len(shape))

    return pl.pallas_call(
        _ffn_kernel,
        grid=(batch, nt),
        in_specs=[
            pl.BlockSpec((1, BF16_ROWS, D_MODEL), lambda b, t: (b, jnp.maximum(t * hb - 1, 0), 0)),
            pl.BlockSpec((1, tm, D_MODEL), lambda b, t: (b, t, 0)),
            pl.BlockSpec((1, BF16_ROWS, D_MODEL), lambda b, t: (b, jnp.minimum((t + 1) * hb, last_halo), 0)),
            pl.BlockSpec((1, tm, D_MODEL), lambda b, t: (b, t, 0)),
            pl.BlockSpec((1, 1, tm, PLE_DIM), lambda b, t: (layer, b, t, 0)),
            whole((D_MODEL, D_FF)), whole((D_MODEL, D_FF)), whole((3, D_FF)), whole((1, D_FF)),
            whole((D_FF, D_MODEL)), whole((1, D_MODEL)), whole((D_MODEL, D_MODEL)),
            whole((1, D_MODEL)), whole((PLE_DIM, D_MODEL)),
        ],
        out_specs=pl.BlockSpec((1, tm, D_MODEL), lambda b, t: (b, t, 0)),
        out_shape=jax.ShapeDtypeStruct((batch, seq, D_MODEL), F32),
        scratch_shapes=[pltpu.VMEM((tm + 2 * BF16_ROWS, D_MODEL), MXU_DTYPE)],
        compiler_params=pltpu.CompilerParams(dimension_semantics=("arbitrary", "arbitrary"),
                                             vmem_limit_bytes=VMEM_LIMIT),
    )(h2, h2, h2, r1, p, wg, wu, cw, cb, wd, png, pwg, pbg, pwp)


def _final_kernel(r_ref, g_ref, out_ref):
    out_ref[...] = _rms(r_ref[...], g_ref[...])


def _final_norm(r2d, g):
    t = r2d.shape[0]
    tm = TOKEN_TILE
    return pl.pallas_call(
        _final_kernel,
        grid=(t // tm,),
        in_specs=[pl.BlockSpec((tm, D_MODEL), lambda i: (i, 0)),
                  pl.BlockSpec((1, D_MODEL), lambda i: (0, 0))],
        out_specs=pl.BlockSpec((tm, D_MODEL), lambda i: (i, 0)),
        out_shape=jax.ShapeDtypeStruct((t, D_MODEL), F32),
        compiler_params=pltpu.CompilerParams(dimension_semantics=("arbitrary",)),
    )(r2d, g)


def _row(v):
    return v.reshape(1, -1).astype(F32)


def _gate_lane_row(v):
    return jnp.zeros((1, LANES), F32).at[0, 8:16].set(v.reshape(-1).astype(F32))


def _lru_gate_weights(wa, ba, wx, bx):
    nblk = LRU_WIDTH // LANES
    per = LANES // LRU_BLOCK

    def bd(w):
        w = w.reshape(nblk, per, LRU_BLOCK, LRU_BLOCK)
        eye = jnp.eye(per, dtype=w.dtype)
        return jnp.einsum('npcd,pq->npcqd', w, eye).reshape(nblk, LANES, LANES)

    wcat = jnp.concatenate([bd(wa[0]), bd(wx[0]), bd(wa[1]), bd(wx[1])], axis=2)
    bcat = jnp.concatenate([ba[0].reshape(nblk, 1, LANES), bx[0].reshape(nblk, 1, LANES),
                            ba[1].reshape(nblk, 1, LANES), bx[1].reshape(nblk, 1, LANES)], axis=2)
    return wcat.astype(MXU_DTYPE), bcat.astype(F32)


def kernel(x, p, norm1_g, w_in, dn_conv_w, dn_a_log, dn_dt_bias, dn_norm_g, lru_conv_w, lru_conv_b,
           lru_wa, lru_ba, lru_wx, lru_bx, lru_lambda, lru_norm_g, w_out, norm2_g, ffn_wg, ffn_wu,
           ffn_conv_w, ffn_conv_b, ffn_wd, ple_norm_g, ple_wg, ple_bg, ple_wp, final_g):
    batch, seq, _ = x.shape
    depth = w_in.shape[0]
    assert seq % TOKEN_TILE == 0 and seq % CHUNK == 0 and (batch * seq) % TOKEN_TILE == 0
    nblk = LRU_WIDTH // LANES
    r = x.reshape(batch * seq, D_MODEL)
    for i in range(depth):
        w_perm = jnp.concatenate(
            [w_in[i][:, :BETA_OFF], w_in[i][:, LX_OFF:], w_in[i][:, BETA_OFF:LX_OFF],
             jnp.zeros((D_MODEL, LANES - (LX_OFF - BETA_OFF)), w_in.dtype)], axis=1).astype(MXU_DTYPE)
        proj, pack = _inproj(r, _row(norm1_g[i]), w_perm,
                             _gate_lane_row(dn_a_log[i]), _gate_lane_row(dn_dt_bias[i]))
        dn = _deltanet(proj, pack, dn_conv_w[i], _row(dn_norm_g[i]), batch, seq)
        wgate, bgate = _lru_gate_weights(lru_wa[i], lru_ba[i], lru_wx[i], lru_bx[i])
        lam = jnp.transpose(lru_lambda[i].reshape(2, nblk, LANES), (1, 0, 2)).astype(F32)
        ly = _rglru(proj, lru_conv_w[i], _row(lru_conv_b[i]), wgate, bgate, lam, batch, seq)
        r1, h2 = _outproj(r, dn, ly, _row(lru_norm_g[i]), w_out[i].astype(MXU_DTYPE), _row(norm2_g[i]))
        r = _ffn(h2.reshape(batch, seq, D_MODEL), r1.reshape(batch, seq, D_MODEL), p, i,
                 ffn_wg[i].astype(MXU_DTYPE), ffn_wu[i].astype(MXU_DTYPE), ffn_conv_w[i],
                 _row(ffn_conv_b[i]), ffn_wd[i].astype(MXU_DTYPE), _row(ple_norm_g[i]),
                 ple_wg[i].astype(MXU_DTYPE), _row(ple_bg[i]), ple_wp[i].astype(MXU_DTYPE))
        r = r.reshape(batch * seq, D_MODEL)
    return _final_norm(r, _row(final_g)).reshape(batch, seq, D_MODEL)
```

```python
import functools

import jax
import numpy as np
import jax.numpy as jnp
from jax import lax
from jax.experimental import pallas as pl
from jax.experimental.pallas import tpu as pltpu

D_MODEL = 1024
DN_HEADS = 4
DN_HEAD_DIM = 128
DN_WIDTH = DN_HEADS * DN_HEAD_DIM
LRU_WIDTH = 512
LRU_BLOCKS = 8
LRU_BLOCK = LRU_WIDTH // LRU_BLOCKS
LRU_C = 8.0
D_FF = 2816
PLE_DIM = 256
EPS = 1e-6
Z_OFF = 3 * DN_WIDTH
BETA_OFF = 4 * DN_WIDTH
LX_OFF = BETA_OFF + 4 * DN_HEADS
IN_COLS = LX_OFF + 2 * LRU_WIDTH

LANES = 128
SUBLANES = 8
BF16_ROWS = 16
CHUNK = 128
PHASE_A_GROUP = 8
MAIN_COLS = 4 * DN_WIDTH + 2 * LRU_WIDTH
N_MAIN_BLK = MAIN_COLS // LANES
PROJ_COLS = MAIN_COLS + LANES
TOKEN_TILE = 512
FF_CHUNK = D_FF // 2
VMEM_LIMIT = 56 * 1024 * 1024

MXU_DTYPE = jnp.bfloat16
F32 = jnp.float32


def _mm(a, b):
    return jnp.dot(a.astype(MXU_DTYPE), b.astype(MXU_DTYPE), preferred_element_type=F32)


def _mm_nt(a, b):
    return lax.dot_general(a.astype(MXU_DTYPE), b.astype(MXU_DTYPE), (((1,), (1,)), ((), ())),
                           preferred_element_type=F32)


def _rms(x, g):
    return x * lax.rsqrt(jnp.mean(x * x, axis=-1, keepdims=True) + EPS) * g


def _row_iota(shape):
    return lax.broadcasted_iota(jnp.int32, shape, 0)


def _col_iota(shape):
    return lax.broadcasted_iota(jnp.int32, shape, 1)


def _seq_conv(x, w, left):
    n = x.shape[0]
    row = _row_iota(x.shape)
    out = None
    for j in range(w.shape[0]):
        off = j - left
        if off == 0:
            xs = x
        elif off < 0:
            xs = jnp.where(row >= -off, pltpu.roll(x, -off, 0), 0.0)
        else:
            xs = jnp.where(row < n - off, pltpu.roll(x, n - off, 0), 0.0)
        term = xs * w[j:j + 1, :]
        out = term if out is None else out + term
    return out


def _inproj_kernel(r_ref, g_ref, w_ref, alog_ref, dtb_ref, proj_ref, pack_ref):
    h = _rms(r_ref[...], g_ref[...])
    res = _mm(h, w_ref[...])
    for j in range(N_MAIN_BLK):
        proj_ref[j] = res[:, j * LANES:(j + 1) * LANES]
    gates = res[:, MAIN_COLS:]
    tm = gates.shape[0]
    beta = jax.nn.sigmoid(gates)
    g = -jnp.exp(alog_ref[...]) * jax.nn.softplus(gates + dtb_ref[...])
    sq = (CHUNK, CHUNK)
    tri = jnp.concatenate([(_col_iota(sq) <= _row_iota(sq)).astype(MXU_DTYPE),
                           (_col_iota(sq) >= _row_iota(sq)).astype(MXU_DTYPE)], axis=0)
    lane = _col_iota((CHUNK, LANES))
    for c in range(tm // CHUNK):
        gc = g[c * CHUNK:(c + 1) * CHUNK]
        p1 = gc.astype(MXU_DTYPE)
        r1 = gc - p1.astype(F32)
        p2 = r1.astype(MXU_DTYPE)
        p3 = (r1 - p2.astype(F32)).astype(MXU_DTYPE)
        cs = jnp.dot(tri, jnp.concatenate([p1, p2, p3], axis=1), preferred_element_type=F32)
        cs = cs[:, :LANES] + cs[:, LANES:2 * LANES] + cs[:, 2 * LANES:]
        prefix, suffix = cs[:CHUNK], cs[CHUNK:]
        pk = jnp.where(lane < 8, beta[c * CHUNK:(c + 1) * CHUNK],
                       jnp.where(lane < 12, prefix, suffix))
        for hd in range(DN_HEADS):
            pack_ref[hd, c * CHUNK:(c + 1) * CHUNK, :] = pk if hd == 0 else pltpu.roll(pk, LANES - hd, 1)


def _inproj(r2d, g1, w, alog, dtb):
    t = r2d.shape[0]
    tm = TOKEN_TILE
    return pl.pallas_call(
        _inproj_kernel,
        name="inproj",
        grid=(t // tm,),
        in_specs=[
            pl.BlockSpec((tm, D_MODEL), lambda i: (i, 0)),
            pl.BlockSpec((1, D_MODEL), lambda i: (0, 0)),
            pl.BlockSpec((D_MODEL, PROJ_COLS), lambda i: (0, 0)),
            pl.BlockSpec((1, LANES), lambda i: (0, 0)),
            pl.BlockSpec((1, LANES), lambda i: (0, 0)),
        ],
        out_specs=[
            pl.BlockSpec((N_MAIN_BLK, tm, LANES), lambda i: (0, i, 0)),
            pl.BlockSpec((DN_HEADS, tm, LANES), lambda i: (0, i, 0)),
        ],
        out_shape=[
            jax.ShapeDtypeStruct((N_MAIN_BLK, t, LANES), F32),
            jax.ShapeDtypeStruct((DN_HEADS, t, LANES), F32),
        ],
        compiler_params=pltpu.CompilerParams(dimension_semantics=("arbitrary",),
                                             vmem_limit_bytes=VMEM_LIMIT),
    )(r2d, g1, w, alog, dtb)


N_TRI_MASKS = 6


def _tri_masks():
    i = np.arange(CHUNK)[:, None]
    j = np.arange(CHUNK)[None, :]
    out = np.zeros((2, N_TRI_MASKS, CHUNK, CHUNK), np.float32)
    for d in range(2):
        hi, lo = (j, i) if d == 1 else (i, j)
        out[d, 0] = (hi // SUBLANES == lo // SUBLANES) & (hi > lo)
        s, lvl = SUBLANES, 1
        while s < CHUNK:
            out[d, lvl] = (hi // (2 * s) == lo // (2 * s)) & ((hi // s) % 2 == 1) & ((lo // s) % 2 == 0)
            s, lvl = 2 * s, lvl + 1
        out[d, N_TRI_MASKS - 1] = hi >= lo
    return out


def _unit_tri_inverse_minus_eye(a_list, mask_of):
    idx = range(len(a_list))
    x = [a_list[i] * mask_of(i, 0) for i in idx]
    y2 = [_mm(x[i], x[i]) for i in idx]
    y4 = [_mm(y2[i], y2[i]) for i in idx]
    xy2 = [_mm(x[i], y2[i]) for i in idx]
    m1 = [y2[i] - x[i] - xy2[i] for i in idx]
    m1y4 = [_mm(m1[i], y4[i]) for i in idx]
    n = [m1[i] + y4[i] + m1y4[i] for i in idx]
    for lvl in range(1, N_TRI_MASKS - 1):
        b = [a_list[i] * mask_of(i, lvl) for i in idx]
        p = [b[i] + _mm(n[i], b[i]) for i in idx]
        pn = [_mm(p[i], n[i]) for i in idx]
        n = [n[i] - p[i] - pn[i] for i in idx]
    return n


def _dn_kernel(q_ref, k_ref, v_ref, z_ref, pack_ref, cwq_ref, cwk_ref, cwv_ref, ng_ref, mask_ref,
               out_ref, qs_ref, ks_ref, vs_ref, u_ref, wq_ref, attn_ref, kdt_ref, cd_ref, o_ref):
    seq = q_ref.shape[1]
    nc = seq // CHUNK

    def l2n(x):
        return x * lax.rsqrt(jnp.sum(x * x, axis=-1, keepdims=True) + EPS)

    def conv_silu(ref, cw_ref):
        y = _seq_conv(ref[0], cw_ref[...], 2)
        return y * jax.nn.sigmoid(y)

    qs_ref[...] = l2n(conv_silu(q_ref, cwq_ref)) * (DN_HEAD_DIM ** -0.5)
    ks_ref[...] = l2n(conv_silu(k_ref, cwk_ref))
    vs_ref[...] = conv_silu(v_ref, cwv_ref)
    o_ref[...] = jnp.zeros_like(o_ref)

    sq = (CHUNK, CHUNK)

    def phase_a(gi, carry):
        chunks = []
        for c in range(PHASE_A_GROUP):
            n = gi * PHASE_A_GROUP + c
            off = pl.multiple_of(n * CHUNK, CHUNK)
            chunks.append((n, off, qs_ref[pl.ds(off, CHUNK), :], ks_ref[pl.ds(off, CHUNK), :]))
        kqs = [_mm_nt(jnp.concatenate([kc, qc], axis=0), kc) for (_, _, qc, kc) in chunks]
        inst = []
        for (n, off, qc, kc), kq in zip(chunks, kqs):
            pk = pack_ref[0, pl.ds(off, CHUNK), :]
            for d in range(2):
                beta = jnp.broadcast_to(pk[:, 4 * d:4 * d + 1], sq)
                gc = jnp.broadcast_to(pk[:, 8 + 4 * d:9 + 4 * d], sq)
                glast = jnp.broadcast_to(gc[0:1, :] if d == 1 else gc[CHUNK - 1:CHUNK, :], sq)
                e_in = jnp.exp(gc)
                decay = jnp.exp(jnp.minimum(gc - gc.T, 0.0)) * mask_ref[d, N_TRI_MASKS - 1]
                attn_ref[d, n] = (kq[CHUNK:] * decay).astype(attn_ref.dtype)
                kdt_ref[d, n] = (kc * jnp.exp(glast - gc)).T.astype(kdt_ref.dtype)
                cd_ref[d, n] = jnp.exp(glast[:SUBLANES, :])
                wq_ref[d, n, CHUNK:, :] = (qc * e_in).astype(wq_ref.dtype)
                vc = vs_ref[pl.ds(off, CHUNK), :]
                rhs = jnp.concatenate([vc * beta, kc * (beta * e_in)], axis=1)
                inst.append((d, n, off, beta * kq[:CHUNK] * decay, rhs))
        n_inv = _unit_tri_inverse_minus_eye([t[3] for t in inst], lambda i, m: mask_ref[inst[i][0], m])
        uws = [_mm(n_inv[i], inst[i][4]) for i in range(len(inst))]
        for (d, n, off, _, rhs), nrhs in zip(inst, uws):
            uw = rhs + nrhs
            u_ref[d, pl.ds(off, CHUNK), :] = uw[:, :DN_HEAD_DIM]
            wq_ref[d, n, :CHUNK, :] = uw[:, DN_HEAD_DIM:].astype(wq_ref.dtype)
        return carry

    lax.fori_loop(0, nc // PHASE_A_GROUP, phase_a, 0)

    def phase_b(i, states):
        ns = (i, nc - 1 - i)
        offs = [pl.multiple_of(n * CHUNK, CHUNK) for n in ns]
        wqs = [_mm(wq_ref[d, ns[d]], states[d]) for d in range(2)]
        v_new = [u_ref[d, pl.ds(offs[d], CHUNK), :] - wqs[d][:CHUNK] for d in range(2)]
        intra = [_mm(attn_ref[d, ns[d]], v_new[d]) for d in range(2)]
        upd = [_mm(kdt_ref[d, ns[d]], v_new[d]) for d in range(2)]
        for d in range(2):
            o_ref[pl.ds(offs[d], CHUNK), :] += wqs[d][CHUNK:] + intra[d]
        return tuple(states[d] * cd_ref[d, ns[d]][0:1, :] + upd[d] for d in range(2))

    zero = jnp.zeros((DN_HEAD_DIM, DN_HEAD_DIM), F32)
    lax.fori_loop(0, nc, phase_b, (zero, zero))

    z = z_ref[0]
    out_ref[0] = _rms(o_ref[...], ng_ref[...]) * (z * jax.nn.sigmoid(z))


def _deltanet(proj, pack, conv_w, norm_g, batch, seq):
    nc = seq // CHUNK
    assert nc % PHASE_A_GROUP == 0
    blk = (1, seq, LANES)
    masks = jnp.asarray(_tri_masks())

    def col(j0):
        return pl.BlockSpec(blk, lambda b, h: (j0 + h, b, 0))

    def cw(j0):
        return pl.BlockSpec((4, LANES), lambda b, h: (0, j0 + h))

    return pl.pallas_call(
        _dn_kernel,
        name="deltanet",
        grid=(batch, DN_HEADS),
        in_specs=[col(0), col(DN_HEADS), col(2 * DN_HEADS), col(3 * DN_HEADS),
                  pl.BlockSpec(blk, lambda b, h: (h, b, 0)),
                  cw(0), cw(DN_HEADS), cw(2 * DN_HEADS),
                  pl.BlockSpec((1, LANES), lambda b, h: (0, 0)),
                  pl.BlockSpec(masks.shape, lambda b, h: (0, 0, 0, 0))],
        out_specs=pl.BlockSpec(blk, lambda b, h: (h, b, 0)),
        out_shape=jax.ShapeDtypeStruct((DN_HEADS, batch * seq, LANES), F32),
        scratch_shapes=[
            pltpu.VMEM((seq, LANES), F32),
            pltpu.VMEM((seq, LANES), F32),
            pltpu.VMEM((seq, LANES), F32),
            pltpu.VMEM((2, seq, LANES), F32),
            pltpu.VMEM((2, nc, 2 * CHUNK, LANES), MXU_DTYPE),
            pltpu.VMEM((2, nc, CHUNK, CHUNK), MXU_DTYPE),
            pltpu.VMEM((2, nc, LANES, CHUNK), MXU_DTYPE),
            pltpu.VMEM((2, nc, SUBLANES, LANES), F32),
            pltpu.VMEM((seq, LANES), F32),
        ],
        compiler_params=pltpu.CompilerParams(dimension_semantics=("arbitrary", "arbitrary"),
                                             vmem_limit_bytes=VMEM_LIMIT),
    )(proj, proj, proj, proj, pack, conv_w, conv_w, conv_w, norm_g, masks)


def _lru_kernel(lx_ref, lg_ref, cw_ref, cb_ref, wg_ref, bg_ref, lam_ref, y_ref,
                a_ref, b_ref, h_ref):
    seq = lx_ref.shape[1]
    ngrp = seq // SUBLANES
    xc = _seq_conv(lx_ref[0], cw_ref[...], 2) + cb_ref[...]
    gates = _mm(xc, wg_ref[0]) + bg_ref[0]
    sub = _row_iota((seq, LANES)) % SUBLANES
    for d in range(2):
        rev = d == 1
        r = jax.nn.sigmoid(gates[:, (2 * d) * LANES:(2 * d + 1) * LANES])
        ig = jax.nn.sigmoid(gates[:, (2 * d + 1) * LANES:(2 * d + 2) * LANES])
        log_a = -LRU_C * r * jax.nn.softplus(-lam_ref[0, d:d + 1, :])
        a = jnp.exp(log_a)
        b = jnp.sqrt(-jnp.tanh(log_a) * (a * a + 1.0)) * (ig * xc)
        s = 1
        while s < SUBLANES:
            sh = seq - s if rev else s
            keep = (sub < SUBLANES - s) if rev else (sub >= s)
            b = jnp.where(keep, a * pltpu.roll(b, sh, 0) + b, b)
            a = jnp.where(keep, a * pltpu.roll(a, sh, 0), a)
            s *= 2
        a_ref[d] = a
        b_ref[d] = b

    zero = jnp.zeros((1, LANES), F32)

    def fwd_step(i, hprev):
        off = pl.multiple_of(i * SUBLANES, SUBLANES)
        hg = b_ref[0, pl.ds(off, SUBLANES), :] + a_ref[0, pl.ds(off, SUBLANES), :] * hprev
        h_ref[pl.ds(off, SUBLANES), :] = hg
        return hg[SUBLANES - 1:SUBLANES, :]

    def bwd_step(i, hnext):
        off = pl.multiple_of((ngrp - 1 - i) * SUBLANES, SUBLANES)
        hg = b_ref[1, pl.ds(off, SUBLANES), :] + a_ref[1, pl.ds(off, SUBLANES), :] * hnext
        h_ref[pl.ds(off, SUBLANES), :] += hg
        return hg[0:1, :]

    lax.fori_loop(0, ngrp, fwd_step, zero, unroll=8)
    lax.fori_loop(0, ngrp, bwd_step, zero, unroll=8)
    y_ref[0] = jax.nn.gelu(lg_ref[0]) * h_ref[...]


def _rglru(proj, conv_w, conv_b, wgate, bgate, lam, batch, seq):
    nblk = LRU_WIDTH // LANES
    blk = (1, seq, LANES)
    j_lx = 4 * DN_HEADS
    j_lg = j_lx + nblk
    return pl.pallas_call(
        _lru_kernel,
        name="rglru",
        grid=(batch, nblk),
        in_specs=[
            pl.BlockSpec(blk, lambda b, c: (j_lx + c, b, 0)),
            pl.BlockSpec(blk, lambda b, c: (j_lg + c, b, 0)),
            pl.BlockSpec((4, LANES), lambda b, c: (0, c)),
            pl.BlockSpec((1, LANES), lambda b, c: (0, c)),
            pl.BlockSpec((1, LANES, 4 * LANES), lambda b, c: (c, 0, 0)),
            pl.BlockSpec((1, 1, 4 * LANES), lambda b, c: (c, 0, 0)),
            pl.BlockSpec((1, 2, LANES), lambda b, c: (c, 0, 0)),
        ],
        out_specs=pl.BlockSpec(blk, lambda b, c: (c, b, 0)),
        out_shape=jax.ShapeDtypeStruct((nblk, batch * seq, LANES), F32),
        scratch_shapes=[
            pltpu.VMEM((2, seq, LANES), F32),
            pltpu.VMEM((2, seq, LANES), F32),
            pltpu.VMEM((seq, LANES), F32),
        ],
        compiler_params=pltpu.CompilerParams(dimension_semantics=("arbitrary", "arbitrary"),
                                             vmem_limit_bytes=VMEM_LIMIT),
    )(proj, proj, conv_w, conv_b, wgate, bgate, lam)


def _outproj_kernel(r_ref, dn_ref, ly_ref, lg_ref, w_ref, g2_ref, r1_ref, h2_ref):
    dn = jnp.concatenate([dn_ref[j] for j in range(dn_ref.shape[0])], axis=1)
    ly = jnp.concatenate([ly_ref[j] for j in range(ly_ref.shape[0])], axis=1)
    mix = jnp.concatenate([dn, _rms(ly, lg_ref[...])], axis=1)
    r1 = r_ref[...] + _mm(mix, w_ref[...])
    r1_ref[...] = r1
    h2_ref[...] = _rms(r1, g2_ref[...]).astype(h2_ref.dtype)


def _outproj(r2d, dn, ly, lru_g, w_out, g2):
    t = r2d.shape[0]
    tm = TOKEN_TILE
    nb = LRU_WIDTH // LANES
    return pl.pallas_call(
        _outproj_kernel,
        name="outproj",
        grid=(t // tm,),
        in_specs=[
            pl.BlockSpec((tm, D_MODEL), lambda i: (i, 0)),
            pl.BlockSpec((DN_HEADS, tm, LANES), lambda i: (0, i, 0)),
            pl.BlockSpec((nb, tm, LANES), lambda i: (0, i, 0)),
            pl.BlockSpec((1, LRU_WIDTH), lambda i: (0, 0)),
            pl.BlockSpec((D_MODEL, D_MODEL), lambda i: (0, 0)),
            pl.BlockSpec((1, D_MODEL), lambda i: (0, 0)),
        ],
        out_specs=[
            pl.BlockSpec((tm, D_MODEL), lambda i: (i, 0)),
            pl.BlockSpec((tm, D_MODEL), lambda i: (i, 0)),
        ],
        out_shape=[
            jax.ShapeDtypeStruct((t, D_MODEL), F32),
            jax.ShapeDtypeStruct((t, D_MODEL), MXU_DTYPE),
        ],
        compiler_params=pltpu.CompilerParams(dimension_semantics=("arbitrary",),
                                             vmem_limit_bytes=VMEM_LIMIT),
    )(r2d, dn, ly, lru_g, w_out, g2)


def _ffn_kernel(hp_ref, hm_ref, hn_ref, r1_ref, p_ref, wg_ref, wu_ref, cw_ref, cb_ref, wd_ref,
                png_ref, pwg_ref, pbg_ref, pwp_ref, out_ref, hcat_ref):
    t = pl.program_id(1)
    tm = hm_ref.shape[1]
    halo = BF16_ROWS
    hcat_ref[0:halo, :] = jnp.where(t > 0, hp_ref[0], jnp.zeros_like(hp_ref[0]))
    hcat_ref[halo:halo + tm, :] = hm_ref[0]
    hcat_ref[halo + tm:, :] = jnp.where(t < pl.num_programs(1) - 1, hn_ref[0], jnp.zeros_like(hn_ref[0]))
    rows = tm + 2 * halo
    acc = r1_ref[0]
    for c in range(D_FF // FF_CHUNK):
        cs = slice(c * FF_CHUNK, (c + 1) * FF_CHUNK)
        gfull = jnp.dot(hcat_ref[...], wg_ref[:, cs], preferred_element_type=F32)
        gate = (pltpu.roll(gfull, 1, 0)[halo:halo + tm] * cw_ref[0:1, cs]
                + gfull[halo:halo + tm] * cw_ref[1:2, cs]
                + pltpu.roll(gfull, rows - 1, 0)[halo:halo + tm] * cw_ref[2:3, cs]
                + cb_ref[:, cs])
        up = jnp.dot(hm_ref[0], wu_ref[:, cs], preferred_element_type=F32)
        acc = acc + _mm(jax.nn.gelu(gate) * up, wd_ref[cs, :])
    pgate = jax.nn.sigmoid(_mm(_rms(acc, png_ref[...]), pwg_ref[...]) + pbg_ref[...])
    out_ref[0] = acc + pgate * _mm(p_ref[0, 0], pwp_ref[...])


def _ffn(h2, r1, p, layer, wg, wu, cw, cb, wd, png, pwg, pbg, pwp):
    batch, seq, _ = h2.shape
    tm = TOKEN_TILE
    nt = seq // tm
    hb = tm // BF16_ROWS
    last_halo = seq // BF16_ROWS - 1

    def whole(shape):
        return pl.BlockSpec(shape, lambda b, t: (0,) * len(shape))

    return pl.pallas_call(
        _ffn_kernel,
        name="ffn",
        grid=(batch, nt),
        in_specs=[
            pl.BlockSpec((1, BF16_ROWS, D_MODEL), lambda b, t: (b, jnp.maximum(t * hb - 1, 0), 0)),
            pl.BlockSpec((1, tm, D_MODEL), lambda b, t: (b, t, 0)),
            pl.BlockSpec((1, BF16_ROWS, D_MODEL), lambda b, t: (b, jnp.minimum((t + 1) * hb, last_halo), 0)),
            pl.BlockSpec((1, tm, D_MODEL), lambda b, t: (b, t, 0)),
            pl.BlockSpec((1, 1, tm, PLE_DIM), lambda b, t: (layer, b, t, 0)),
            whole((D_MODEL, D_FF)), whole((D_MODEL, D_FF)), whole((3, D_FF)), whole((1, D_FF)),
            whole((D_FF, D_MODEL)), whole((1, D_MODEL)), whole((D_MODEL, D_MODEL)),
            whole((1, D_MODEL)), whole((PLE_DIM, D_MODEL)),
        ],
        out_specs=pl.BlockSpec((1, tm, D_MODEL), lambda b, t: (b, t, 0)),
        out_shape=jax.ShapeDtypeStruct((batch, seq, D_MODEL), F32),
        scratch_shapes=[pltpu.VMEM((tm + 2 * BF16_ROWS, D_MODEL), MXU_DTYPE)],
        compiler_params=pltpu.CompilerParams(dimension_semantics=("arbitrary", "arbitrary"),
                                             vmem_limit_bytes=VMEM_LIMIT),
    )(h2, h2, h2, r1, p, wg, wu, cw, cb, wd, png, pwg, pbg, pwp)


def _final_kernel(r_ref, g_ref, out_ref):
    out_ref[...] = _rms(r_ref[...], g_ref[...])


def _final_norm(r2d, g):
    t = r2d.shape[0]
    tm = TOKEN_TILE
    return pl.pallas_call(
        _final_kernel,
        name="final_norm",
        grid=(t // tm,),
        in_specs=[pl.BlockSpec((tm, D_MODEL), lambda i: (i, 0)),
                  pl.BlockSpec((1, D_MODEL), lambda i: (0, 0))],
        out_specs=pl.BlockSpec((tm, D_MODEL), lambda i: (i, 0)),
        out_shape=jax.ShapeDtypeStruct((t, D_MODEL), F32),
        compiler_params=pltpu.CompilerParams(dimension_semantics=("arbitrary",)),
    )(r2d, g)


def _row(v):
    return v.reshape(1, -1).astype(F32)


def _gate_lane_row(v):
    return jnp.zeros((1, LANES), F32).at[0, 8:16].set(v.reshape(-1).astype(F32))


def _lru_gate_weights(wa, ba, wx, bx):
    nblk = LRU_WIDTH // LANES
    per = LANES // LRU_BLOCK

    def bd(w):
        w = w.reshape(nblk, per, LRU_BLOCK, LRU_BLOCK)
        eye = jnp.eye(per, dtype=w.dtype)
        return jnp.einsum('npcd,pq->npcqd', w, eye).reshape(nblk, LANES, LANES)

    wcat = jnp.concatenate([bd(wa[0]), bd(wx[0]), bd(wa[1]), bd(wx[1])], axis=2)
    bcat = jnp.concatenate([ba[0].reshape(nblk, 1, LANES), bx[0].reshape(nblk, 1, LANES),
                            ba[1].reshape(nblk, 1, LANES), bx[1].reshape(nblk, 1, LANES)], axis=2)
    return wcat.astype(MXU_DTYPE), bcat.astype(F32)


def kernel(x, p, norm1_g, w_in, dn_conv_w, dn_a_log, dn_dt_bias, dn_norm_g, lru_conv_w, lru_conv_b,
           lru_wa, lru_ba, lru_wx, lru_bx, lru_lambda, lru_norm_g, w_out, norm2_g, ffn_wg, ffn_wu,
           ffn_conv_w, ffn_conv_b, ffn_wd, ple_norm_g, ple_wg, ple_bg, ple_wp, final_g):
    batch, seq, _ = x.shape
    depth = w_in.shape[0]
    assert seq % TOKEN_TILE == 0 and seq % CHUNK == 0 and (batch * seq) % TOKEN_TILE == 0
    nblk = LRU_WIDTH // LANES
    r = x.reshape(batch * seq, D_MODEL)
    for i in range(depth):
        w_perm = jnp.concatenate(
            [w_in[i][:, :BETA_OFF], w_in[i][:, LX_OFF:], w_in[i][:, BETA_OFF:LX_OFF],
             jnp.zeros((D_MODEL, LANES - (LX_OFF - BETA_OFF)), w_in.dtype)], axis=1).astype(MXU_DTYPE)
        proj, pack = _inproj(r, _row(norm1_g[i]), w_perm,
                             _gate_lane_row(dn_a_log[i]), _gate_lane_row(dn_dt_bias[i]))
        dn = _deltanet(proj, pack, dn_conv_w[i], _row(dn_norm_g[i]), batch, seq)
        wgate, bgate = _lru_gate_weights(lru_wa[i], lru_ba[i], lru_wx[i], lru_bx[i])
        lam = jnp.transpose(lru_lambda[i].reshape(2, nblk, LANES), (1, 0, 2)).astype(F32)
        ly = _rglru(proj, lru_conv_w[i], _row(lru_conv_b[i]), wgate, bgate, lam, batch, seq)
        r1, h2 = _outproj(r, dn, ly, _row(lru_norm_g[i]), w_out[i].astype(MXU_DTYPE), _row(norm2_g[i]))
        r = _ffn(h2.reshape(batch, seq, D_MODEL), r1.reshape(batch, seq, D_MODEL), p, i,
                 ffn_wg[i].astype(MXU_DTYPE), ffn_wu[i].astype(MXU_DTYPE), ffn_conv_w[i],
                 _row(ffn_conv_b[i]), ffn_wd[i].astype(MXU_DTYPE), _row(ple_norm_g[i]),
                 ple_wg[i].astype(MXU_DTYPE), _row(ple_bg[i]), ple_wp[i].astype(MXU_DTYPE))
        r = r.reshape(batch * seq, D_MODEL)
    return _final_norm(r, _row(final_g)).reshape(batch, seq, D_MODEL)
```

```python
import functools

import jax
import numpy as np
import jax.numpy as jnp
from jax import lax
from jax.experimental import pallas as pl
from jax.experimental.pallas import tpu as pltpu

D_MODEL = 1024
DN_HEADS = 4
DN_HEAD_DIM = 128
DN_WIDTH = DN_HEADS * DN_HEAD_DIM
LRU_WIDTH = 512
LRU_BLOCKS = 8
LRU_BLOCK = LRU_WIDTH // LRU_BLOCKS
LRU_C = 8.0
D_FF = 2816
PLE_DIM = 256
EPS = 1e-6
Z_OFF = 3 * DN_WIDTH
BETA_OFF = 4 * DN_WIDTH
LX_OFF = BETA_OFF + 4 * DN_HEADS
IN_COLS = LX_OFF + 2 * LRU_WIDTH

LANES = 128
SUBLANES = 8
BF16_ROWS = 16
CHUNK = 128
PACKT_ROWS = 8
PHASE_A_GROUP = 8
MAIN_COLS = 4 * DN_WIDTH + 2 * LRU_WIDTH
N_MAIN_BLK = MAIN_COLS // LANES
PROJ_COLS = MAIN_COLS + LANES
TOKEN_TILE = 512
FF_CHUNK = D_FF // 2
VMEM_LIMIT = 56 * 1024 * 1024

MXU_DTYPE = jnp.bfloat16
F32 = jnp.float32


def _mm(a, b):
    return jnp.dot(a.astype(MXU_DTYPE), b.astype(MXU_DTYPE), preferred_element_type=F32)


def _mm_nt(a, b):
    return lax.dot_general(a.astype(MXU_DTYPE), b.astype(MXU_DTYPE), (((1,), (1,)), ((), ())),
                           preferred_element_type=F32)


def _rms(x, g):
    return x * lax.rsqrt(jnp.mean(x * x, axis=-1, keepdims=True) + EPS) * g


def _row_iota(shape):
    return lax.broadcasted_iota(jnp.int32, shape, 0)


def _col_iota(shape):
    return lax.broadcasted_iota(jnp.int32, shape, 1)


def _sigmoid(x):
    return 0.5 * jnp.tanh(0.5 * x) + 0.5


def _seq_conv(x, w, left, pad_ref):
    n = x.shape[0]
    zeros = jnp.zeros((SUBLANES, x.shape[1]), F32)
    pad_ref[0:SUBLANES, :] = zeros
    pad_ref[SUBLANES + n:, :] = zeros
    pad_ref[SUBLANES:SUBLANES + n, :] = x
    out = None
    for j in range(w.shape[0]):
        term = pad_ref[SUBLANES + j - left:SUBLANES + j - left + n, :] * w[j:j + 1, :]
        out = term if out is None else out + term
    return out


def _inproj_kernel(r_ref, g_ref, w_ref, alog_ref, dtb_ref, proj_ref, pack_ref, packt_ref):
    h = _rms(r_ref[...], g_ref[...])
    res = _mm(h, w_ref[...])
    for j in range(N_MAIN_BLK):
        proj_ref[j] = res[:, j * LANES:(j + 1) * LANES].astype(proj_ref.dtype)
    gates = res[:, MAIN_COLS:]
    tm = gates.shape[0]
    beta = _sigmoid(gates)
    g = -jnp.exp(alog_ref[...]) * jax.nn.softplus(gates + dtb_ref[...])
    sq = (CHUNK, CHUNK)
    tri = jnp.concatenate([(_col_iota(sq) <= _row_iota(sq)).astype(MXU_DTYPE),
                           (_col_iota(sq) >= _row_iota(sq)).astype(MXU_DTYPE)], axis=0)
    lane = _col_iota((CHUNK, LANES))
    for c in range(tm // CHUNK):
        gc = g[c * CHUNK:(c + 1) * CHUNK]
        p1 = gc.astype(MXU_DTYPE)
        r1 = gc - p1.astype(F32)
        p2 = r1.astype(MXU_DTYPE)
        p3 = (r1 - p2.astype(F32)).astype(MXU_DTYPE)
        cs = jnp.dot(tri, jnp.concatenate([p1, p2, p3], axis=1), preferred_element_type=F32)
        cs = cs[:, :LANES] + cs[:, LANES:2 * LANES] + cs[:, 2 * LANES:]
        prefix, suffix = cs[:CHUNK], cs[CHUNK:]
        pk = jnp.where(lane < 8, beta[c * CHUNK:(c + 1) * CHUNK],
                       jnp.where(lane < 12, prefix, suffix))
        for hd in range(DN_HEADS):
            pack_ref[hd, c * CHUNK:(c + 1) * CHUNK, :] = pk if hd == 0 else pltpu.roll(pk, LANES - hd, 1)
        pkt = pk.T
        for hd in range(DN_HEADS):
            packt_ref[hd, :, c * CHUNK:(c + 1) * CHUNK] = jnp.concatenate(
                [pkt[8 + hd:9 + hd], pkt[12 + hd:13 + hd], jnp.zeros((PACKT_ROWS - 2, CHUNK), F32)], axis=0)


def _inproj(r2d, g1, w, alog, dtb):
    t = r2d.shape[0]
    tm = TOKEN_TILE
    return pl.pallas_call(
        _inproj_kernel,
        name="inproj",
        grid=(t // tm,),
        in_specs=[
            pl.BlockSpec((tm, D_MODEL), lambda i: (i, 0)),
            pl.BlockSpec((1, D_MODEL), lambda i: (0, 0)),
            pl.BlockSpec((D_MODEL, PROJ_COLS), lambda i: (0, 0)),
            pl.BlockSpec((1, LANES), lambda i: (0, 0)),
            pl.BlockSpec((1, LANES), lambda i: (0, 0)),
        ],
        out_specs=[
            pl.BlockSpec((N_MAIN_BLK, tm, LANES), lambda i: (0, i, 0)),
            pl.BlockSpec((DN_HEADS, tm, LANES), lambda i: (0, i, 0)),
            pl.BlockSpec((DN_HEADS, PACKT_ROWS, tm), lambda i: (0, 0, i)),
        ],
        out_shape=[
            jax.ShapeDtypeStruct((N_MAIN_BLK, t, LANES), MXU_DTYPE),
            jax.ShapeDtypeStruct((DN_HEADS, t, LANES), F32),
            jax.ShapeDtypeStruct((DN_HEADS, PACKT_ROWS, t), F32),
        ],
        compiler_params=pltpu.CompilerParams(dimension_semantics=("arbitrary",),
                                             vmem_limit_bytes=VMEM_LIMIT),
    )(r2d, g1, w, alog, dtb)


N_TRI_MASKS = 6


def _tri_masks():
    i = np.arange(CHUNK)[:, None]
    j = np.arange(CHUNK)[None, :]
    out = np.zeros((2, N_TRI_MASKS, CHUNK, CHUNK), np.float32)
    for d in range(2):
        hi, lo = (j, i) if d == 1 else (i, j)
        out[d, 0] = (hi // SUBLANES == lo // SUBLANES) & (hi > lo)
        s, lvl = SUBLANES, 1
        while s < CHUNK:
            out[d, lvl] = (hi // (2 * s) == lo // (2 * s)) & ((hi // s) % 2 == 1) & ((lo // s) % 2 == 0)
            s, lvl = 2 * s, lvl + 1
        out[d, N_TRI_MASKS - 1] = hi >= lo
    return out


def _unit_tri_inverse_minus_eye(a_list, mask_of):
    idx = range(len(a_list))
    x = [a_list[i] * mask_of(i, 0) for i in idx]
    y2 = [_mm(x[i], x[i]) for i in idx]
    y4 = [_mm(y2[i], y2[i]) for i in idx]
    xy2 = [_mm(x[i], y2[i]) for i in idx]
    m1 = [y2[i] - x[i] - xy2[i] for i in idx]
    m1y4 = [_mm(m1[i], y4[i]) for i in idx]
    n = [m1[i] + y4[i] + m1y4[i] for i in idx]
    for lvl in range(1, N_TRI_MASKS - 1):
        b = [a_list[i] * mask_of(i, lvl) for i in idx]
        p = [b[i] + _mm(n[i], b[i]) for i in idx]
        pn = [_mm(p[i], n[i]) for i in idx]
        n = [n[i] - p[i] - pn[i] for i in idx]
    return n


def _dn_kernel(q_ref, k_ref, v_ref, z_ref, pack_ref, packt_ref, cwq_ref, cwk_ref, cwv_ref, ng_ref,
               mask_ref, out_ref, pad_ref, qs_ref, ks_ref, vs_ref, c_ref, nmq_ref, cd_ref, o_ref):
    seq = q_ref.shape[1]
    nc = seq // CHUNK

    def l2n(x):
        return x * lax.rsqrt(jnp.sum(x * x, axis=-1, keepdims=True) + EPS)

    def conv_silu(ref, cw_ref):
        y = _seq_conv(ref[0].astype(F32), cw_ref[...], 2, pad_ref)
        return y * _sigmoid(y)

    qs_ref[...] = l2n(conv_silu(q_ref, cwq_ref)) * (DN_HEAD_DIM ** -0.5)
    ks_ref[...] = l2n(conv_silu(k_ref, cwk_ref))
    vs_ref[...] = conv_silu(v_ref, cwv_ref)
    o_ref[...] = jnp.zeros_like(o_ref)

    sq = (CHUNK, CHUNK)
    eye = (_row_iota(sq) == _col_iota(sq)).astype(MXU_DTYPE)

    def phase_a(gi, carry):
        chunks = []
        for c in range(PHASE_A_GROUP):
            n = gi * PHASE_A_GROUP + c
            off = pl.multiple_of(n * CHUNK, CHUNK)
            chunks.append((n, off, qs_ref[pl.ds(off, CHUNK), :], ks_ref[pl.ds(off, CHUNK), :]))
        kqs = [_mm_nt(jnp.concatenate([kc.astype(MXU_DTYPE), qc.astype(MXU_DTYPE), eye], axis=0), kc)
               for (_, _, qc, kc) in chunks]
        inst = []
        for (n, off, qc, kc), kq in zip(chunks, kqs):
            pk = pack_ref[0, pl.ds(off, CHUNK), :]
            vc = vs_ref[pl.ds(off, CHUNK), :]
            for d in range(2):
                beta = jnp.broadcast_to(pk[:, 4 * d:4 * d + 1], sq)
                gc = jnp.broadcast_to(pk[:, 8 + 4 * d:9 + 4 * d], sq)
                gc_row = packt_ref[0, d:d + 1, pl.ds(off, CHUNK)]
                glast = gc[0:1, :] if d == 1 else gc[CHUNK - 1:CHUNK, :]
                e_in = jnp.exp(gc)
                decay = jnp.exp(jnp.minimum(gc - gc_row, 0.0)) * mask_ref[d, N_TRI_MASKS - 1]
                attn = kq[CHUNK:2 * CHUNK] * decay
                kdt = kq[2 * CHUNK:] * jnp.exp(glast - gc_row)
                cd_ref[d, n] = jnp.exp(jnp.broadcast_to(glast, (SUBLANES, LANES)))
                rhs = jnp.concatenate([vc * beta, kc * (beta * e_in)], axis=1)
                inst.append((d, n, off, beta * kq[:CHUNK] * decay, rhs, attn, kdt, qc * e_in))
        idx = range(len(inst))
        n_inv = _unit_tri_inverse_minus_eye([t[3] for t in inst], lambda i, m: mask_ref[inst[i][0], m])
        uw = [inst[i][4] + _mm(n_inv[i], inst[i][4]) for i in idx]
        kuw = [_mm(inst[i][6], uw[i]) for i in idx]
        auw = [_mm(inst[i][5], uw[i]) for i in idx]
        for i in idx:
            d, n, off = inst[i][:3]
            c_ref[d, n] = kuw[i][:, :DN_HEAD_DIM]
            nmq_ref[d, n, :CHUNK, :] = (-kuw[i][:, DN_HEAD_DIM:]).astype(nmq_ref.dtype)
            nmq_ref[d, n, CHUNK:, :] = (inst[i][7] - auw[i][:, DN_HEAD_DIM:]).astype(nmq_ref.dtype)
            o_ref[pl.ds(off, CHUNK), :] += auw[i][:, :DN_HEAD_DIM]
        return carry

    lax.fori_loop(0, nc // PHASE_A_GROUP, phase_a, 0)

    def phase_b(i, states):
        ns = (i, nc - 1 - i)
        rs = [_mm(nmq_ref[d, ns[d]], states[d]) for d in range(2)]
        for d in range(2):
            o_ref[pl.ds(pl.multiple_of(ns[d] * CHUNK, CHUNK), CHUNK), :] += rs[d][CHUNK:]
        return tuple(states[d] * cd_ref[d, ns[d]][0:1, :] + rs[d][:CHUNK] + c_ref[d, ns[d]] for d in range(2))

    zero = jnp.zeros((DN_HEAD_DIM, DN_HEAD_DIM), F32)
    lax.fori_loop(0, nc, phase_b, (zero, zero))

    z = z_ref[0].astype(F32)
    out_ref[0] = (_rms(o_ref[...], ng_ref[...]) * (z * _sigmoid(z))).astype(out_ref.dtype)


def _deltanet(proj, pack, packt, conv_w, norm_g, batch, seq):
    nc = seq // CHUNK
    assert nc % PHASE_A_GROUP == 0
    blk = (1, seq, LANES)
    masks = jnp.asarray(_tri_masks())

    def col(j0):
        return pl.BlockSpec(blk, lambda b, h: (j0 + h, b, 0))

    def cw(j0):
        return pl.BlockSpec((4, LANES), lambda b, h: (0, j0 + h))

    return pl.pallas_call(
        _dn_kernel,
        name="deltanet",
        grid=(batch, DN_HEADS),
        in_specs=[col(0), col(DN_HEADS), col(2 * DN_HEADS), col(3 * DN_HEADS),
                  pl.BlockSpec(blk, lambda b, h: (h, b, 0)),
                  pl.BlockSpec((1, PACKT_ROWS, seq), lambda b, h: (h, 0, b)),
                  cw(0), cw(DN_HEADS), cw(2 * DN_HEADS),
                  pl.BlockSpec((1, LANES), lambda b, h: (0, 0)),
                  pl.BlockSpec(masks.shape, lambda b, h: (0, 0, 0, 0))],
        out_specs=pl.BlockSpec(blk, lambda b, h: (h, b, 0)),
        out_shape=jax.ShapeDtypeStruct((DN_HEADS, batch * seq, LANES), MXU_DTYPE),
        scratch_shapes=[
            pltpu.VMEM((seq + 2 * SUBLANES, LANES), F32),
            pltpu.VMEM((seq, LANES), F32),
            pltpu.VMEM((seq, LANES), F32),
            pltpu.VMEM((seq, LANES), F32),
            pltpu.VMEM((2, nc, DN_HEAD_DIM, DN_HEAD_DIM), F32),
            pltpu.VMEM((2, nc, 2 * CHUNK, LANES), MXU_DTYPE),
            pltpu.VMEM((2, nc, SUBLANES, LANES), F32),
            pltpu.VMEM((seq, LANES), F32),
        ],
        compiler_params=pltpu.CompilerParams(dimension_semantics=("arbitrary", "arbitrary"),
                                             vmem_limit_bytes=VMEM_LIMIT),
    )(proj, proj, proj, proj, pack, packt, conv_w, conv_w, conv_w, norm_g, masks)


def _lru_kernel(lx_ref, lg_ref, cw_ref, cb_ref, wg_ref, bg_ref, lam_ref, y_ref,
                pad_ref, a_ref, b_ref, h_ref):
    seq = lx_ref.shape[1]
    ngrp = seq // SUBLANES
    xc = _seq_conv(lx_ref[0].astype(F32), cw_ref[...], 2, pad_ref) + cb_ref[...]
    gates = _mm(xc, wg_ref[0]) + bg_ref[0]
    sub = _row_iota((seq, LANES)) % SUBLANES
    for d in range(2):
        rev = d == 1
        r = _sigmoid(gates[:, (2 * d) * LANES:(2 * d + 1) * LANES])
        ig = _sigmoid(gates[:, (2 * d + 1) * LANES:(2 * d + 2) * LANES])
        log_a = -LRU_C * r * jax.nn.softplus(-lam_ref[0, d:d + 1, :])
        a = jnp.exp(log_a)
        b = jnp.sqrt(-jnp.tanh(log_a) * (a * a + 1.0)) * (ig * xc)
        s = 1
        while s < SUBLANES:
            sh = seq - s if rev else s
            keep = (sub < SUBLANES - s) if rev else (sub >= s)
            b = jnp.where(keep, a * pltpu.roll(b, sh, 0) + b, b)
            a = jnp.where(keep, a * pltpu.roll(a, sh, 0), a)
            s *= 2
        a_ref[d] = a
        b_ref[d] = b

    zero = jnp.zeros((1, LANES), F32)

    def fwd_step(i, hprev):
        off = pl.multiple_of(i * SUBLANES, SUBLANES)
        hg = b_ref[0, pl.ds(off, SUBLANES), :] + a_ref[0, pl.ds(off, SUBLANES), :] * hprev
        h_ref[pl.ds(off, SUBLANES), :] = hg
        return hg[SUBLANES - 1:SUBLANES, :]

    def bwd_step(i, hnext):
        off = pl.multiple_of((ngrp - 1 - i) * SUBLANES, SUBLANES)
        hg = b_ref[1, pl.ds(off, SUBLANES), :] + a_ref[1, pl.ds(off, SUBLANES), :] * hnext
        h_ref[pl.ds(off, SUBLANES), :] += hg
        return hg[0:1, :]

    lax.fori_loop(0, ngrp, fwd_step, zero, unroll=8)
    lax.fori_loop(0, ngrp, bwd_step, zero, unroll=8)
    y_ref[0] = (jax.nn.gelu(lg_ref[0].astype(F32)) * h_ref[...]).astype(y_ref.dtype)


def _rglru(proj, conv_w, conv_b, wgate, bgate, lam, batch, seq):
    nblk = LRU_WIDTH // LANES
    blk = (1, seq, LANES)
    j_lx = 4 * DN_HEADS
    j_lg = j_lx + nblk
    return pl.pallas_call(
        _lru_kernel,
        name="rglru",
        grid=(batch, nblk),
        in_specs=[
            pl.BlockSpec(blk, lambda b, c: (j_lx + c, b, 0)),
            pl.BlockSpec(blk, lambda b, c: (j_lg + c, b, 0)),
            pl.BlockSpec((4, LANES), lambda b, c: (0, c)),
            pl.BlockSpec((1, LANES), lambda b, c: (0, c)),
            pl.BlockSpec((1, LANES, 4 * LANES), lambda b, c: (c, 0, 0)),
            pl.BlockSpec((1, 1, 4 * LANES), lambda b, c: (c, 0, 0)),
            pl.BlockSpec((1, 2, LANES), lambda b, c: (c, 0, 0)),
        ],
        out_specs=pl.BlockSpec(blk, lambda b, c: (c, b, 0)),
        out_shape=jax.ShapeDtypeStruct((nblk, batch * seq, LANES), MXU_DTYPE),
        scratch_shapes=[
            pltpu.VMEM((seq + 2 * SUBLANES, LANES), F32),
            pltpu.VMEM((2, seq, LANES), F32),
            pltpu.VMEM((2, seq, LANES), F32),
            pltpu.VMEM((seq, LANES), F32),
        ],
        compiler_params=pltpu.CompilerParams(dimension_semantics=("arbitrary", "arbitrary"),
                                             vmem_limit_bytes=VMEM_LIMIT),
    )(proj, proj, conv_w, conv_b, wgate, bgate, lam)


def _outproj_kernel(r_ref, dn_ref, ly_ref, lg_ref, w_ref, g2_ref, r1_ref, h2_ref):
    dn = jnp.concatenate([dn_ref[j] for j in range(dn_ref.shape[0])], axis=1)
    ly = jnp.concatenate([ly_ref[j] for j in range(ly_ref.shape[0])], axis=1).astype(F32)
    mix = jnp.concatenate([dn, _rms(ly, lg_ref[...]).astype(dn.dtype)], axis=1)
    r1 = r_ref[...] + _mm(mix, w_ref[...])
    r1_ref[...] = r1
    h2_ref[...] = _rms(r1, g2_ref[...]).astype(h2_ref.dtype)


def _outproj(r2d, dn, ly, lru_g, w_out, g2):
    t = r2d.shape[0]
    tm = TOKEN_TILE
    nb = LRU_WIDTH // LANES
    return pl.pallas_call(
        _outproj_kernel,
        name="outproj",
        grid=(t // tm,),
        in_specs=[
            pl.BlockSpec((tm, D_MODEL), lambda i: (i, 0)),
            pl.BlockSpec((DN_HEADS, tm, LANES), lambda i: (0, i, 0)),
            pl.BlockSpec((nb, tm, LANES), lambda i: (0, i, 0)),
            pl.BlockSpec((1, LRU_WIDTH), lambda i: (0, 0)),
            pl.BlockSpec((D_MODEL, D_MODEL), lambda i: (0, 0)),
            pl.BlockSpec((1, D_MODEL), lambda i: (0, 0)),
        ],
        out_specs=[
            pl.BlockSpec((tm, D_MODEL), lambda i: (i, 0)),
            pl.BlockSpec((tm, D_MODEL), lambda i: (i, 0)),
        ],
        out_shape=[
            jax.ShapeDtypeStruct((t, D_MODEL), F32),
            jax.ShapeDtypeStruct((t, D_MODEL), MXU_DTYPE),
        ],
        compiler_params=pltpu.CompilerParams(dimension_semantics=("arbitrary",),
                                             vmem_limit_bytes=VMEM_LIMIT),
    )(r2d, dn, ly, lru_g, w_out, g2)


def _ffn_kernel(hp_ref, hm_ref, hn_ref, r1_ref, p_ref, wg_ref, wu_ref, cw_ref, cb_ref, wd_ref,
                png_ref, pwg_ref, pbg_ref, pwp_ref, out_ref, hcat_ref):
    t = pl.program_id(1)
    tm = hm_ref.shape[1]
    halo = BF16_ROWS
    hcat_ref[0:halo, :] = jnp.where(t > 0, hp_ref[0], jnp.zeros_like(hp_ref[0]))
    hcat_ref[halo:halo + tm, :] = hm_ref[0]
    hcat_ref[halo + tm:, :] = jnp.where(t < pl.num_programs(1) - 1, hn_ref[0], jnp.zeros_like(hn_ref[0]))
    rows = tm + 2 * halo
    acc = r1_ref[0]
    for c in range(D_FF // FF_CHUNK):
        cs = slice(c * FF_CHUNK, (c + 1) * FF_CHUNK)
        gfull = jnp.dot(hcat_ref[...], wg_ref[:, cs], preferred_element_type=F32)
        gate = (pltpu.roll(gfull, 1, 0)[halo:halo + tm] * cw_ref[0:1, cs]
                + gfull[halo:halo + tm] * cw_ref[1:2, cs]
                + pltpu.roll(gfull, rows - 1, 0)[halo:halo + tm] * cw_ref[2:3, cs]
                + cb_ref[:, cs])
        up = jnp.dot(hm_ref[0], wu_ref[:, cs], preferred_element_type=F32)
        acc = acc + _mm(jax.nn.gelu(gate) * up, wd_ref[cs, :])
    pgate = _sigmoid(_mm(_rms(acc, png_ref[...]), pwg_ref[...]) + pbg_ref[...])
    out_ref[0] = acc + pgate * _mm(p_ref[0, 0], pwp_ref[...])


def _ffn(h2, r1, p, layer, wg, wu, cw, cb, wd, png, pwg, pbg, pwp):
    batch, seq, _ = h2.shape
    tm = TOKEN_TILE
    nt = seq // tm
    hb = tm // BF16_ROWS
    last_halo = seq // BF16_ROWS - 1

    def whole(shape):
        return pl.BlockSpec(shape, lambda b, t: (0,) * len(shape))

    return pl.pallas_call(
        _ffn_kernel,
        name="ffn",
        grid=(batch, nt),
        in_specs=[
            pl.BlockSpec((1, BF16_ROWS, D_MODEL), lambda b, t: (b, jnp.maximum(t * hb - 1, 0), 0)),
            pl.BlockSpec((1, tm, D_MODEL), lambda b, t: (b, t, 0)),
            pl.BlockSpec((1, BF16_ROWS, D_MODEL), lambda b, t: (b, jnp.minimum((t + 1) * hb, last_halo), 0)),
            pl.BlockSpec((1, tm, D_MODEL), lambda b, t: (b, t, 0)),
            pl.BlockSpec((1, 1, tm, PLE_DIM), lambda b, t: (layer, b, t, 0)),
            whole((D_MODEL, D_FF)), whole((D_MODEL, D_FF)), whole((3, D_FF)), whole((1, D_FF)),
            whole((D_FF, D_MODEL)), whole((1, D_MODEL)), whole((D_MODEL, D_MODEL)),
            whole((1, D_MODEL)), whole((PLE_DIM, D_MODEL)),
        ],
        out_specs=pl.BlockSpec((1, tm, D_MODEL), lambda b, t: (b, t, 0)),
        out_shape=jax.ShapeDtypeStruct((batch, seq, D_MODEL), F32),
        scratch_shapes=[pltpu.VMEM((tm + 2 * BF16_ROWS, D_MODEL), MXU_DTYPE)],
        compiler_params=pltpu.CompilerParams(dimension_semantics=("arbitrary", "arbitrary"),
                                             vmem_limit_bytes=VMEM_LIMIT),
    )(h2, h2, h2, r1, p, wg, wu, cw, cb, wd, png, pwg, pbg, pwp)


def _final_kernel(r_ref, g_ref, out_ref):
    out_ref[...] = _rms(r_ref[...], g_ref[...])


def _final_norm(r2d, g):
    t = r2d.shape[0]
    tm = TOKEN_TILE
    return pl.pallas_call(
        _final_kernel,
        name="final_norm",
        grid=(t // tm,),
        in_specs=[pl.BlockSpec((tm, D_MODEL), lambda i: (i, 0)),
                  pl.BlockSpec((1, D_MODEL), lambda i: (0, 0))],
        out_specs=pl.BlockSpec((tm, D_MODEL), lambda i: (i, 0)),
        out_shape=jax.ShapeDtypeStruct((t, D_MODEL), F32),
        compiler_params=pltpu.CompilerParams(dimension_semantics=("arbitrary",)),
    )(r2d, g)


def _row(v):
    return v.reshape(1, -1).astype(F32)


def _gate_lane_row(v):
    return jnp.zeros((1, LANES), F32).at[0, 8:16].set(v.reshape(-1).astype(F32))


def _lru_gate_weights(wa, ba, wx, bx):
    nblk = LRU_WIDTH // LANES
    per = LANES // LRU_BLOCK

    def bd(w):
        w = w.reshape(nblk, per, LRU_BLOCK, LRU_BLOCK)
        eye = jnp.eye(per, dtype=w.dtype)
        return jnp.einsum('npcd,pq->npcqd', w, eye).reshape(nblk, LANES, LANES)

    wcat = jnp.concatenate([bd(wa[0]), bd(wx[0]), bd(wa[1]), bd(wx[1])], axis=2)
    bcat = jnp.concatenate([ba[0].reshape(nblk, 1, LANES), bx[0].reshape(nblk, 1, LANES),
                            ba[1].reshape(nblk, 1, LANES), bx[1].reshape(nblk, 1, LANES)], axis=2)
    return wcat.astype(MXU_DTYPE), bcat.astype(F32)


def kernel(x, p, norm1_g, w_in, dn_conv_w, dn_a_log, dn_dt_bias, dn_norm_g, lru_conv_w, lru_conv_b,
           lru_wa, lru_ba, lru_wx, lru_bx, lru_lambda, lru_norm_g, w_out, norm2_g, ffn_wg, ffn_wu,
           ffn_conv_w, ffn_conv_b, ffn_wd, ple_norm_g, ple_wg, ple_bg, ple_wp, final_g):
    batch, seq, _ = x.shape
    depth = w_in.shape[0]
    assert seq % TOKEN_TILE == 0 and seq % CHUNK == 0 and (batch * seq) % TOKEN_TILE == 0
    nblk = LRU_WIDTH // LANES
    r = x.reshape(batch * seq, D_MODEL)
    for i in range(depth):
        w_perm = jnp.concatenate(
            [w_in[i][:, :BETA_OFF], w_in[i][:, LX_OFF:], w_in[i][:, BETA_OFF:LX_OFF],
             jnp.zeros((D_MODEL, LANES - (LX_OFF - BETA_OFF)), w_in.dtype)], axis=1).astype(MXU_DTYPE)
        proj, pack, packt = _inproj(r, _row(norm1_g[i]), w_perm,
                                    _gate_lane_row(dn_a_log[i]), _gate_lane_row(dn_dt_bias[i]))
        dn = _deltanet(proj, pack, packt, dn_conv_w[i], _row(dn_norm_g[i]), batch, seq)
        wgate, bgate = _lru_gate_weights(lru_wa[i], lru_ba[i], lru_wx[i], lru_bx[i])
        lam = jnp.transpose(lru_lambda[i].reshape(2, nblk, LANES), (1, 0, 2)).astype(F32)
        ly = _rglru(proj, lru_conv_w[i], _row(lru_conv_b[i]), wgate, bgate, lam, batch, seq)
        r1, h2 = _outproj(r, dn, ly, _row(lru_norm_g[i]), w_out[i].astype(MXU_DTYPE), _row(norm2_g[i]))
        r = _ffn(h2.reshape(batch, seq, D_MODEL), r1.reshape(batch, seq, D_MODEL), p, i,
                 ffn_wg[i].astype(MXU_DTYPE), ffn_wu[i].astype(MXU_DTYPE), ffn_conv_w[i],
                 _row(ffn_conv_b[i]), ffn_wd[i].astype(MXU_DTYPE), _row(ple_norm_g[i]),
                 ple_wg[i].astype(MXU_DTYPE), _row(ple_bg[i]), ple_wp[i].astype(MXU_DTYPE))
        r = r.reshape(batch * seq, D_MODEL)
    return _final_norm(r, _row(final_g)).reshape(batch, seq, D_MODEL)
```

```python
import functools

import jax
import numpy as np
import jax.numpy as jnp
from jax import lax
from jax.experimental import pallas as pl
from jax.experimental.pallas import tpu as pltpu

D_MODEL = 1024
DN_HEADS = 4
DN_HEAD_DIM = 128
DN_WIDTH = DN_HEADS * DN_HEAD_DIM
LRU_WIDTH = 512
LRU_BLOCKS = 8
LRU_BLOCK = LRU_WIDTH // LRU_BLOCKS
LRU_C = 8.0
D_FF = 2816
PLE_DIM = 256
EPS = 1e-6
Z_OFF = 3 * DN_WIDTH
BETA_OFF = 4 * DN_WIDTH
LX_OFF = BETA_OFF + 4 * DN_HEADS
IN_COLS = LX_OFF + 2 * LRU_WIDTH

LANES = 128
SUBLANES = 8
BF16_ROWS = 16
CHUNK = 128
PACKT_ROWS = 8
SCAN_FANOUT = 4
SCAN_BOTTOM_ROWS = 32
PHASE_A_GROUP = 8
MAIN_COLS = 4 * DN_WIDTH + 2 * LRU_WIDTH
N_MAIN_BLK = MAIN_COLS // LANES
PROJ_COLS = MAIN_COLS + LANES
TOKEN_TILE = 512
FF_CHUNK = D_FF // 2
VMEM_LIMIT = 56 * 1024 * 1024

MXU_DTYPE = jnp.bfloat16
F32 = jnp.float32


def _mm(a, b):
    return jnp.dot(a.astype(MXU_DTYPE), b.astype(MXU_DTYPE), preferred_element_type=F32)


def _mm_nt(a, b):
    return lax.dot_general(a.astype(MXU_DTYPE), b.astype(MXU_DTYPE), (((1,), (1,)), ((), ())),
                           preferred_element_type=F32)


def _rms(x, g):
    return x * lax.rsqrt(jnp.mean(x * x, axis=-1, keepdims=True) + EPS) * g


def _row_iota(shape):
    return lax.broadcasted_iota(jnp.int32, shape, 0)


def _col_iota(shape):
    return lax.broadcasted_iota(jnp.int32, shape, 1)


def _sigmoid(x):
    return 0.5 * jnp.tanh(0.5 * x) + 0.5


def _seq_conv(x, w, left, pad_ref):
    n = x.shape[0]
    zeros = jnp.zeros((SUBLANES, x.shape[1]), F32)
    pad_ref[0:SUBLANES, :] = zeros
    pad_ref[SUBLANES + n:, :] = zeros
    pad_ref[SUBLANES:SUBLANES + n, :] = x
    out = None
    for j in range(w.shape[0]):
        term = pad_ref[SUBLANES + j - left:SUBLANES + j - left + n, :] * w[j:j + 1, :]
        out = term if out is None else out + term
    return out


def _inproj_kernel(r_ref, g_ref, w_ref, alog_ref, dtb_ref, proj_ref, pack_ref, packt_ref):
    h = _rms(r_ref[...], g_ref[...])
    res = _mm(h, w_ref[...])
    for j in range(N_MAIN_BLK):
        proj_ref[j] = res[:, j * LANES:(j + 1) * LANES].astype(proj_ref.dtype)
    gates = res[:, MAIN_COLS:]
    tm = gates.shape[0]
    beta = _sigmoid(gates)
    g = -jnp.exp(alog_ref[...]) * jax.nn.softplus(gates + dtb_ref[...])
    sq = (CHUNK, CHUNK)
    tri = jnp.concatenate([(_col_iota(sq) <= _row_iota(sq)).astype(MXU_DTYPE),
                           (_col_iota(sq) >= _row_iota(sq)).astype(MXU_DTYPE)], axis=0)
    lane = _col_iota((CHUNK, LANES))
    for c in range(tm // CHUNK):
        gc = g[c * CHUNK:(c + 1) * CHUNK]
        p1 = gc.astype(MXU_DTYPE)
        r1 = gc - p1.astype(F32)
        p2 = r1.astype(MXU_DTYPE)
        p3 = (r1 - p2.astype(F32)).astype(MXU_DTYPE)
        cs = jnp.dot(tri, jnp.concatenate([p1, p2, p3], axis=1), preferred_element_type=F32)
        cs = cs[:, :LANES] + cs[:, LANES:2 * LANES] + cs[:, 2 * LANES:]
        prefix, suffix = cs[:CHUNK], cs[CHUNK:]
        pk = jnp.where(lane < 8, beta[c * CHUNK:(c + 1) * CHUNK],
                       jnp.where(lane < 12, prefix, suffix))
        for hd in range(DN_HEADS):
            pack_ref[hd, c * CHUNK:(c + 1) * CHUNK, :] = pk if hd == 0 else pltpu.roll(pk, LANES - hd, 1)
        pkt = pk.T
        for hd in range(DN_HEADS):
            packt_ref[hd, :, c * CHUNK:(c + 1) * CHUNK] = jnp.concatenate(
                [pkt[8 + hd:9 + hd], pkt[12 + hd:13 + hd], jnp.zeros((PACKT_ROWS - 2, CHUNK), F32)], axis=0)


def _inproj(r2d, g1, w, alog, dtb):
    t = r2d.shape[0]
    tm = TOKEN_TILE
    return pl.pallas_call(
        _inproj_kernel,
        name="inproj",
        grid=(t // tm,),
        in_specs=[
            pl.BlockSpec((tm, D_MODEL), lambda i: (i, 0)),
            pl.BlockSpec((1, D_MODEL), lambda i: (0, 0)),
            pl.BlockSpec((D_MODEL, PROJ_COLS), lambda i: (0, 0)),
            pl.BlockSpec((1, LANES), lambda i: (0, 0)),
            pl.BlockSpec((1, LANES), lambda i: (0, 0)),
        ],
        out_specs=[
            pl.BlockSpec((N_MAIN_BLK, tm, LANES), lambda i: (0, i, 0)),
            pl.BlockSpec((DN_HEADS, tm, LANES), lambda i: (0, i, 0)),
            pl.BlockSpec((DN_HEADS, PACKT_ROWS, tm), lambda i: (0, 0, i)),
        ],
        out_shape=[
            jax.ShapeDtypeStruct((N_MAIN_BLK, t, LANES), MXU_DTYPE),
            jax.ShapeDtypeStruct((DN_HEADS, t, LANES), F32),
            jax.ShapeDtypeStruct((DN_HEADS, PACKT_ROWS, t), F32),
        ],
        compiler_params=pltpu.CompilerParams(dimension_semantics=("arbitrary",),
                                             vmem_limit_bytes=VMEM_LIMIT),
    )(r2d, g1, w, alog, dtb)


N_TRI_MASKS = 6


def _tri_masks():
    i = np.arange(CHUNK)[:, None]
    j = np.arange(CHUNK)[None, :]
    out = np.zeros((2, N_TRI_MASKS, CHUNK, CHUNK), np.float32)
    for d in range(2):
        hi, lo = (j, i) if d == 1 else (i, j)
        out[d, 0] = (hi // SUBLANES == lo // SUBLANES) & (hi > lo)
        s, lvl = SUBLANES, 1
        while s < CHUNK:
            out[d, lvl] = (hi // (2 * s) == lo // (2 * s)) & ((hi // s) % 2 == 1) & ((lo // s) % 2 == 0)
            s, lvl = 2 * s, lvl + 1
        out[d, N_TRI_MASKS - 1] = hi >= lo
    return out


def _unit_tri_inverse_minus_eye(a_list, mask_of):
    idx = range(len(a_list))
    x = [a_list[i] * mask_of(i, 0) for i in idx]
    y2 = [_mm(x[i], x[i]) for i in idx]
    y4 = [_mm(y2[i], y2[i]) for i in idx]
    xy2 = [_mm(x[i], y2[i]) for i in idx]
    m1 = [y2[i] - x[i] - xy2[i] for i in idx]
    m1y4 = [_mm(m1[i], y4[i]) for i in idx]
    n = [m1[i] + y4[i] + m1y4[i] for i in idx]
    for lvl in range(1, N_TRI_MASKS - 1):
        b = [a_list[i] * mask_of(i, lvl) for i in idx]
        p = [b[i] + _mm(n[i], b[i]) for i in idx]
        pn = [_mm(p[i], n[i]) for i in idx]
        n = [n[i] - p[i] - pn[i] for i in idx]
    return n


def _dn_kernel(q_ref, k_ref, v_ref, z_ref, pack_ref, packt_ref, cwq_ref, cwk_ref, cwv_ref, ng_ref,
               mask_ref, out_ref, pad_ref, qs_ref, ks_ref, vs_ref, c_ref, nmq_ref, cd_ref, o_ref):
    seq = q_ref.shape[1]
    nc = seq // CHUNK

    def l2n(x):
        return x * lax.rsqrt(jnp.sum(x * x, axis=-1, keepdims=True) + EPS)

    def conv_silu(ref, cw_ref):
        y = _seq_conv(ref[0].astype(F32), cw_ref[...], 2, pad_ref)
        return y * _sigmoid(y)

    qs_ref[...] = l2n(conv_silu(q_ref, cwq_ref)) * (DN_HEAD_DIM ** -0.5)
    ks_ref[...] = l2n(conv_silu(k_ref, cwk_ref))
    vs_ref[...] = conv_silu(v_ref, cwv_ref)
    o_ref[...] = jnp.zeros_like(o_ref)

    sq = (CHUNK, CHUNK)
    eye = (_row_iota(sq) == _col_iota(sq)).astype(MXU_DTYPE)

    def phase_a(gi, carry):
        chunks = []
        for c in range(PHASE_A_GROUP):
            n = gi * PHASE_A_GROUP + c
            off = pl.multiple_of(n * CHUNK, CHUNK)
            chunks.append((n, off, qs_ref[pl.ds(off, CHUNK), :], ks_ref[pl.ds(off, CHUNK), :]))
        kqs = [_mm_nt(jnp.concatenate([kc.astype(MXU_DTYPE), qc.astype(MXU_DTYPE), eye], axis=0), kc)
               for (_, _, qc, kc) in chunks]
        inst = []
        for (n, off, qc, kc), kq in zip(chunks, kqs):
            pk = pack_ref[0, pl.ds(off, CHUNK), :]
            vc = vs_ref[pl.ds(off, CHUNK), :]
            for d in range(2):
                beta = jnp.broadcast_to(pk[:, 4 * d:4 * d + 1], sq)
                gc = jnp.broadcast_to(pk[:, 8 + 4 * d:9 + 4 * d], sq)
                gc_row = packt_ref[0, d:d + 1, pl.ds(off, CHUNK)]
                glast = gc[0:1, :] if d == 1 else gc[CHUNK - 1:CHUNK, :]
                e_in = jnp.exp(gc)
                decay = jnp.exp(jnp.minimum(gc - gc_row, 0.0)) * mask_ref[d, N_TRI_MASKS - 1]
                attn = kq[CHUNK:2 * CHUNK] * decay
                kdt = kq[2 * CHUNK:] * jnp.exp(glast - gc_row)
                cd_ref[d, n] = jnp.exp(jnp.broadcast_to(glast, (SUBLANES, LANES)))
                rhs = jnp.concatenate([vc * beta, kc * (beta * e_in)], axis=1)
                inst.append((d, n, off, beta * kq[:CHUNK] * decay, rhs, attn, kdt, qc * e_in))
        idx = range(len(inst))
        n_inv = _unit_tri_inverse_minus_eye([t[3] for t in inst], lambda i, m: mask_ref[inst[i][0], m])
        uw = [inst[i][4] + _mm(n_inv[i], inst[i][4]) for i in idx]
        kuw = [_mm(inst[i][6], uw[i]) for i in idx]
        auw = [_mm(inst[i][5], uw[i]) for i in idx]
        for i in idx:
            d, n, off = inst[i][:3]
            c_ref[d, n] = kuw[i][:, :DN_HEAD_DIM]
            nmq_ref[d, n, :CHUNK, :] = (-kuw[i][:, DN_HEAD_DIM:]).astype(nmq_ref.dtype)
            nmq_ref[d, n, CHUNK:, :] = (inst[i][7] - auw[i][:, DN_HEAD_DIM:]).astype(nmq_ref.dtype)
            o_ref[pl.ds(off, CHUNK), :] += auw[i][:, :DN_HEAD_DIM]
        return carry

    lax.fori_loop(0, nc // PHASE_A_GROUP, phase_a, 0)

    def phase_b(i, states):
        ns = (i, nc - 1 - i)
        rs = [_mm(nmq_ref[d, ns[d]], states[d]) for d in range(2)]
        for d in range(2):
            o_ref[pl.ds(pl.multiple_of(ns[d] * CHUNK, CHUNK), CHUNK), :] += rs[d][CHUNK:]
        return tuple(states[d] * cd_ref[d, ns[d]][0:1, :] + rs[d][:CHUNK] + c_ref[d, ns[d]] for d in range(2))

    zero = jnp.zeros((DN_HEAD_DIM, DN_HEAD_DIM), F32)
    lax.fori_loop(0, nc, phase_b, (zero, zero))

    z = z_ref[0].astype(F32)
    out_ref[0] = (_rms(o_ref[...], ng_ref[...]) * (z * _sigmoid(z))).astype(out_ref.dtype)


def _deltanet(proj, pack, packt, conv_w, norm_g, batch, seq):
    nc = seq // CHUNK
    assert nc % PHASE_A_GROUP == 0
    blk = (1, seq, LANES)
    masks = jnp.asarray(_tri_masks())

    def col(j0):
        return pl.BlockSpec(blk, lambda b, h: (j0 + h, b, 0))

    def cw(j0):
        return pl.BlockSpec((4, LANES), lambda b, h: (0, j0 + h))

    return pl.pallas_call(
        _dn_kernel,
        name="deltanet",
        grid=(batch, DN_HEADS),
        in_specs=[col(0), col(DN_HEADS), col(2 * DN_HEADS), col(3 * DN_HEADS),
                  pl.BlockSpec(blk, lambda b, h: (h, b, 0)),
                  pl.BlockSpec((1, PACKT_ROWS, seq), lambda b, h: (h, 0, b)),
                  cw(0), cw(DN_HEADS), cw(2 * DN_HEADS),
                  pl.BlockSpec((1, LANES), lambda b, h: (0, 0)),
                  pl.BlockSpec(masks.shape, lambda b, h: (0, 0, 0, 0))],
        out_specs=pl.BlockSpec(blk, lambda b, h: (h, b, 0)),
        out_shape=jax.ShapeDtypeStruct((DN_HEADS, batch * seq, LANES), MXU_DTYPE),
        scratch_shapes=[
            pltpu.VMEM((seq + 2 * SUBLANES, LANES), F32),
            pltpu.VMEM((seq, LANES), F32),
            pltpu.VMEM((seq, LANES), F32),
            pltpu.VMEM((seq, LANES), F32),
            pltpu.VMEM((2, nc, DN_HEAD_DIM, DN_HEAD_DIM), F32),
            pltpu.VMEM((2, nc, 2 * CHUNK, LANES), MXU_DTYPE),
            pltpu.VMEM((2, nc, SUBLANES, LANES), F32),
            pltpu.VMEM((seq, LANES), F32),
        ],
        compiler_params=pltpu.CompilerParams(dimension_semantics=("arbitrary", "arbitrary"),
                                             vmem_limit_bytes=VMEM_LIMIT),
    )(proj, proj, proj, proj, pack, packt, conv_w, conv_w, conv_w, norm_g, masks)


def _scan_level_sizes(seq):
    sizes = [seq]
    while sizes[-1] > SCAN_BOTTOM_ROWS:
        assert sizes[-1] % (SCAN_FANOUT * SUBLANES) == 0
        sizes.append(sizes[-1] // SCAN_FANOUT)
    return sizes


def _lru_kernel(lx_ref, lg_ref, cw_ref, cb_ref, wg_ref, bg_ref, lam_ref, y_ref, pad_ref, h_ref, *lvl):
    nlev = (len(lvl) + 1) // 2
    ab = lvl[:nlev]
    cs = (None,) + tuple(lvl[nlev:])
    seq = lx_ref.shape[1]
    fan = SCAN_FANOUT
    sizes = [r.shape[2] for r in ab]

    xc = _seq_conv(lx_ref[0].astype(F32), cw_ref[...], 2, pad_ref) + cb_ref[...]
    gates = _mm(xc, wg_ref[0]) + bg_ref[0]
    for d in range(2):
        r = _sigmoid(gates[:, (2 * d) * LANES:(2 * d + 1) * LANES])
        ig = _sigmoid(gates[:, (2 * d + 1) * LANES:(2 * d + 2) * LANES])
        log_a = -LRU_C * r * jax.nn.softplus(-lam_ref[0, d:d + 1, :])
        a = jnp.exp(log_a)
        ab[0][d, 0] = a
        ab[0][d, 1] = jnp.sqrt(-jnp.tanh(log_a) * (a * a + 1.0)) * (ig * xc)
        for l in range(1, nlev):
            cs[l][d, 0:SUBLANES, :] = jnp.zeros((SUBLANES, LANES), F32)
            cs[l][d, SUBLANES + sizes[l]:, :] = jnp.zeros((SUBLANES, LANES), F32)

    def slab(l, j):
        return pl.ds(j, sizes[l] // fan, stride=fan)

    for l in range(nlev - 1):
        for d in range(2):
            order = list(range(fan)) if d == 0 else list(range(fan - 1, -1, -1))
            p = ab[l][d, 0, slab(l, order[0]), :]
            h = ab[l][d, 1, slab(l, order[0]), :]
            for j in order[1:]:
                aj = ab[l][d, 0, slab(l, j), :]
                h = aj * h + ab[l][d, 1, slab(l, j), :]
                p = aj * p
                ab[l][d, 0, slab(l, j), :] = p
                ab[l][d, 1, slab(l, j), :] = h
            ab[l + 1][d, 0] = p
            ab[l + 1][d, 1] = h

    top = nlev - 1
    rows = sizes[top]
    row = _row_iota((rows, LANES))
    for d in range(2):
        a = ab[top][d, 0]
        b = ab[top][d, 1]
        s = 1
        while s < rows:
            sh = rows - s if d == 1 else s
            keep = (row < rows - s) if d == 1 else (row >= s)
            b = jnp.where(keep, a * pltpu.roll(b, sh, 0) + b, b)
            a = jnp.where(keep, a * pltpu.roll(a, sh, 0), a)
            s *= 2
        if top == 0:
            ab[0][d, 1] = b
        else:
            cs[top][d, SUBLANES:SUBLANES + rows, :] = b

    for l in range(nlev - 2, -1, -1):
        n = sizes[l] // fan
        carry = [cs[l + 1][d, SUBLANES - 1 + 2 * d:SUBLANES - 1 + 2 * d + n, :] for d in range(2)]
        for j in range(fan):
            true = [ab[l][d, 1, slab(l, j), :] + ab[l][d, 0, slab(l, j), :] * carry[d] for d in range(2)]
            if l == 0:
                h_ref[slab(0, j), :] = true[0] + true[1]
            else:
                for d in range(2):
                    cs[l][d, pl.ds(SUBLANES + j, n, stride=fan), :] = true[d]
    if nlev == 1:
        h_ref[...] = ab[0][0, 1] + ab[0][1, 1]
    y_ref[0] = (jax.nn.gelu(lg_ref[0].astype(F32)) * h_ref[...]).astype(y_ref.dtype)


def _rglru(proj, conv_w, conv_b, wgate, bgate, lam, batch, seq):
    nblk = LRU_WIDTH // LANES
    blk = (1, seq, LANES)
    j_lx = 4 * DN_HEADS
    j_lg = j_lx + nblk
    sizes = _scan_level_sizes(seq)
    return pl.pallas_call(
        _lru_kernel,
        name="rglru",
        grid=(batch, nblk),
        in_specs=[
            pl.BlockSpec(blk, lambda b, c: (j_lx + c, b, 0)),
            pl.BlockSpec(blk, lambda b, c: (j_lg + c, b, 0)),
            pl.BlockSpec((4, LANES), lambda b, c: (0, c)),
            pl.BlockSpec((1, LANES), lambda b, c: (0, c)),
            pl.BlockSpec((1, LANES, 4 * LANES), lambda b, c: (c, 0, 0)),
            pl.BlockSpec((1, 1, 4 * LANES), lambda b, c: (c, 0, 0)),
            pl.BlockSpec((1, 2, LANES), lambda b, c: (c, 0, 0)),
        ],
        out_specs=pl.BlockSpec(blk, lambda b, c: (c, b, 0)),
        out_shape=jax.ShapeDtypeStruct((nblk, batch * seq, LANES), MXU_DTYPE),
        scratch_shapes=(
            [pltpu.VMEM((seq + 2 * SUBLANES, LANES), F32), pltpu.VMEM((seq, LANES), F32)]
            + [pltpu.VMEM((2, 2, m, LANES), F32) for m in sizes]
            + [pltpu.VMEM((2, m + 2 * SUBLANES, LANES), F32) for m in sizes[1:]]),
        compiler_params=pltpu.CompilerParams(dimension_semantics=("arbitrary", "arbitrary"),
                                             vmem_limit_bytes=VMEM_LIMIT),
    )(proj, proj, conv_w, conv_b, wgate, bgate, lam)


def _outproj_kernel(r_ref, dn_ref, ly_ref, lg_ref, w_ref, g2_ref, r1_ref, h2_ref):
    dn = jnp.concatenate([dn_ref[j] for j in range(dn_ref.shape[0])], axis=1)
    ly = jnp.concatenate([ly_ref[j] for j in range(ly_ref.shape[0])], axis=1).astype(F32)
    mix = jnp.concatenate([dn, _rms(ly, lg_ref[...]).astype(dn.dtype)], axis=1)
    r1 = r_ref[...] + _mm(mix, w_ref[...])
    r1_ref[...] = r1
    h2_ref[...] = _rms(r1, g2_ref[...]).astype(h2_ref.dtype)


def _outproj(r2d, dn, ly, lru_g, w_out, g2):
    t = r2d.shape[0]
    tm = TOKEN_TILE
    nb = LRU_WIDTH // LANES
    return pl.pallas_call(
        _outproj_kernel,
        name="outproj",
        grid=(t // tm,),
        in_specs=[
            pl.BlockSpec((tm, D_MODEL), lambda i: (i, 0)),
            pl.BlockSpec((DN_HEADS, tm, LANES), lambda i: (0, i, 0)),
            pl.BlockSpec((nb, tm, LANES), lambda i: (0, i, 0)),
            pl.BlockSpec((1, LRU_WIDTH), lambda i: (0, 0)),
            pl.BlockSpec((D_MODEL, D_MODEL), lambda i: (0, 0)),
            pl.BlockSpec((1, D_MODEL), lambda i: (0, 0)),
        ],
        out_specs=[
            pl.BlockSpec((tm, D_MODEL), lambda i: (i, 0)),
            pl.BlockSpec((tm, D_MODEL), lambda i: (i, 0)),
        ],
        out_shape=[
            jax.ShapeDtypeStruct((t, D_MODEL), F32),
            jax.ShapeDtypeStruct((t, D_MODEL), MXU_DTYPE),
        ],
        compiler_params=pltpu.CompilerParams(dimension_semantics=("arbitrary",),
                                             vmem_limit_bytes=VMEM_LIMIT),
    )(r2d, dn, ly, lru_g, w_out, g2)


def _ffn_kernel(hp_ref, hm_ref, hn_ref, r1_ref, p_ref, wg_ref, wu_ref, cw_ref, cb_ref, wd_ref,
                png_ref, pwg_ref, pbg_ref, pwp_ref, out_ref, hcat_ref):
    t = pl.program_id(1)
    tm = hm_ref.shape[1]
    halo = BF16_ROWS
    hcat_ref[0:halo, :] = jnp.where(t > 0, hp_ref[0], jnp.zeros_like(hp_ref[0]))
    hcat_ref[halo:halo + tm, :] = hm_ref[0]
    hcat_ref[halo + tm:, :] = jnp.where(t < pl.num_programs(1) - 1, hn_ref[0], jnp.zeros_like(hn_ref[0]))
    rows = tm + 2 * halo
    acc = r1_ref[0]
    for c in range(D_FF // FF_CHUNK):
        cs = slice(c * FF_CHUNK, (c + 1) * FF_CHUNK)
        gfull = jnp.dot(hcat_ref[...], wg_ref[:, cs], preferred_element_type=F32)
        gate = (pltpu.roll(gfull, 1, 0)[halo:halo + tm] * cw_ref[0:1, cs]
                + gfull[halo:halo + tm] * cw_ref[1:2, cs]
                + pltpu.roll(gfull, rows - 1, 0)[halo:halo + tm] * cw_ref[2:3, cs]
                + cb_ref[:, cs])
        up = jnp.dot(hm_ref[0], wu_ref[:, cs], preferred_element_type=F32)
        acc = acc + _mm(jax.nn.gelu(gate) * up, wd_ref[cs, :])
    pgate = _sigmoid(_mm(_rms(acc, png_ref[...]), pwg_ref[...]) + pbg_ref[...])
    out_ref[0] = acc + pgate * _mm(p_ref[0, 0], pwp_ref[...])


def _ffn(h2, r1, p, layer, wg, wu, cw, cb, wd, png, pwg, pbg, pwp):
    batch, seq, _ = h2.shape
    tm = TOKEN_TILE
    nt = seq // tm
    hb = tm // BF16_ROWS
    last_halo = seq // BF16_ROWS - 1

    def whole(shape):
        return pl.BlockSpec(shape, lambda b, t: (0,) * len(shape))

    return pl.pallas_call(
        _ffn_kernel,
        name="ffn",
        grid=(batch, nt),
        in_specs=[
            pl.BlockSpec((1, BF16_ROWS, D_MODEL), lambda b, t: (b, jnp.maximum(t * hb - 1, 0), 0)),
            pl.BlockSpec((1, tm, D_MODEL), lambda b, t: (b, t, 0)),
            pl.BlockSpec((1, BF16_ROWS, D_MODEL), lambda b, t: (b, jnp.minimum((t + 1) * hb, last_halo), 0)),
            pl.BlockSpec((1, tm, D_MODEL), lambda b, t: (b, t, 0)),
            pl.BlockSpec((1, 1, tm, PLE_DIM), lambda b, t: (layer, b, t, 0)),
            whole((D_MODEL, D_FF)), whole((D_MODEL, D_FF)), whole((3, D_FF)), whole((1, D_FF)),
            whole((D_FF, D_MODEL)), whole((1, D_MODEL)), whole((D_MODEL, D_MODEL)),
            whole((1, D_MODEL)), whole((PLE_DIM, D_MODEL)),
        ],
        out_specs=pl.BlockSpec((1, tm, D_MODEL), lambda b, t: (b, t, 0)),
        out_shape=jax.ShapeDtypeStruct((batch, seq, D_MODEL), F32),
        scratch_shapes=[pltpu.VMEM((tm + 2 * BF16_ROWS, D_MODEL), MXU_DTYPE)],
        compiler_params=pltpu.CompilerParams(dimension_semantics=("arbitrary", "arbitrary"),
                                             vmem_limit_bytes=VMEM_LIMIT),
    )(h2, h2, h2, r1, p, wg, wu, cw, cb, wd, png, pwg, pbg, pwp)


def _final_kernel(r_ref, g_ref, out_ref):
    out_ref[...] = _rms(r_ref[...], g_ref[...])


def _final_norm(r2d, g):
    t = r2d.shape[0]
    tm = TOKEN_TILE
    return pl.pallas_call(
        _final_kernel,
        name="final_norm",
        grid=(t // tm,),
        in_specs=[pl.BlockSpec((tm, D_MODEL), lambda i: (i, 0)),
                  pl.BlockSpec((1, D_MODEL), lambda i: (0, 0))],
        out_specs=pl.BlockSpec((tm, D_MODEL), lambda i: (i, 0)),
        out_shape=jax.ShapeDtypeStruct((t, D_MODEL), F32),
        compiler_params=pltpu.CompilerParams(dimension_semantics=("arbitrary",)),
    )(r2d, g)


def _row(v):
    return v.reshape(1, -1).astype(F32)


def _gate_lane_row(v):
    return jnp.zeros((1, LANES), F32).at[0, 8:16].set(v.reshape(-1).astype(F32))


def _lru_gate_weights(wa, ba, wx, bx):
    nblk = LRU_WIDTH // LANES
    per = LANES // LRU_BLOCK

    def bd(w):
        w = w.reshape(nblk, per, LRU_BLOCK, LRU_BLOCK)
        eye = jnp.eye(per, dtype=w.dtype)
        return jnp.einsum('npcd,pq->npcqd', w, eye).reshape(nblk, LANES, LANES)

    wcat = jnp.concatenate([bd(wa[0]), bd(wx[0]), bd(wa[1]), bd(wx[1])], axis=2)
    bcat = jnp.concatenate([ba[0].reshape(nblk, 1, LANES), bx[0].reshape(nblk, 1, LANES),
                            ba[1].reshape(nblk, 1, LANES), bx[1].reshape(nblk, 1, LANES)], axis=2)
    return wcat.astype(MXU_DTYPE), bcat.astype(F32)


def kernel(x, p, norm1_g, w_in, dn_conv_w, dn_a_log, dn_dt_bias, dn_norm_g, lru_conv_w, lru_conv_b,
           lru_wa, lru_ba, lru_wx, lru_bx, lru_lambda, lru_norm_g, w_out, norm2_g, ffn_wg, ffn_wu,
           ffn_conv_w, ffn_conv_b, ffn_wd, ple_norm_g, ple_wg, ple_bg, ple_wp, final_g):
    batch, seq, _ = x.shape
    depth = w_in.shape[0]
    assert seq % TOKEN_TILE == 0 and seq % CHUNK == 0 and (batch * seq) % TOKEN_TILE == 0
    nblk = LRU_WIDTH // LANES
    r = x.reshape(batch * seq, D_MODEL)
    for i in range(depth):
        w_perm = jnp.concatenate(
            [w_in[i][:, :BETA_OFF], w_in[i][:, LX_OFF:], w_in[i][:, BETA_OFF:LX_OFF],
             jnp.zeros((D_MODEL, LANES - (LX_OFF - BETA_OFF)), w_in.dtype)], axis=1).astype(MXU_DTYPE)
        proj, pack, packt = _inproj(r, _row(norm1_g[i]), w_perm,
                                    _gate_lane_row(dn_a_log[i]), _gate_lane_row(dn_dt_bias[i]))
        dn = _deltanet(proj, pack, packt, dn_conv_w[i], _row(dn_norm_g[i]), batch, seq)
        wgate, bgate = _lru_gate_weights(lru_wa[i], lru_ba[i], lru_wx[i], lru_bx[i])
        lam = jnp.transpose(lru_lambda[i].reshape(2, nblk, LANES), (1, 0, 2)).astype(F32)
        ly = _rglru(proj, lru_conv_w[i], _row(lru_conv_b[i]), wgate, bgate, lam, batch, seq)
        r1, h2 = _outproj(r, dn, ly, _row(lru_norm_g[i]), w_out[i].astype(MXU_DTYPE), _row(norm2_g[i]))
        r = _ffn(h2.reshape(batch, seq, D_MODEL), r1.reshape(batch, seq, D_MODEL), p, i,
                 ffn_wg[i].astype(MXU_DTYPE), ffn_wu[i].astype(MXU_DTYPE), ffn_conv_w[i],
                 _row(ffn_conv_b[i]), ffn_wd[i].astype(MXU_DTYPE), _row(ple_norm_g[i]),
                 ple_wg[i].astype(MXU_DTYPE), _row(ple_bg[i]), ple_wp[i].astype(MXU_DTYPE))
        r = r.reshape(batch * seq, D_MODEL)
    return _final_norm(r, _row(final_g)).reshape(batch, seq, D_MODEL)
```

```python
import functools

import jax
import numpy as np
import jax.numpy as jnp
from jax import lax
from jax.experimental import pallas as pl
from jax.experimental.pallas import tpu as pltpu

D_MODEL = 1024
DN_HEADS = 4
DN_HEAD_DIM = 128
DN_WIDTH = DN_HEADS * DN_HEAD_DIM
LRU_WIDTH = 512
LRU_BLOCKS = 8
LRU_BLOCK = LRU_WIDTH // LRU_BLOCKS
LRU_C = 8.0
D_FF = 2816
PLE_DIM = 256
EPS = 1e-6
Z_OFF = 3 * DN_WIDTH
BETA_OFF = 4 * DN_WIDTH
LX_OFF = BETA_OFF + 4 * DN_HEADS
IN_COLS = LX_OFF + 2 * LRU_WIDTH

LANES = 128
SUBLANES = 8
BF16_ROWS = 16
CHUNK = 128
PACKT_ROWS = 8
SCAN_FANOUT = 4
SCAN_BOTTOM_ROWS = 32
PHASE_A_GROUP = 8
MAIN_COLS = 4 * DN_WIDTH + 2 * LRU_WIDTH
N_MAIN_BLK = MAIN_COLS // LANES
PROJ_COLS = MAIN_COLS + LANES
TOKEN_TILE = 512
FF_CHUNK = D_FF // 2
VMEM_LIMIT = 56 * 1024 * 1024

MXU_DTYPE = jnp.bfloat16
F32 = jnp.float32


def _mm(a, b):
    return jnp.dot(a.astype(MXU_DTYPE), b.astype(MXU_DTYPE), preferred_element_type=F32)


def _mm_nt(a, b):
    return lax.dot_general(a.astype(MXU_DTYPE), b.astype(MXU_DTYPE), (((1,), (1,)), ((), ())),
                           preferred_element_type=F32)


def _rms(x, g):
    return x * lax.rsqrt(jnp.mean(x * x, axis=-1, keepdims=True) + EPS) * g


def _row_iota(shape):
    return lax.broadcasted_iota(jnp.int32, shape, 0)


def _col_iota(shape):
    return lax.broadcasted_iota(jnp.int32, shape, 1)


def _sigmoid(x):
    return 0.5 * jnp.tanh(0.5 * x) + 0.5


def _seq_conv(x, w, left, pad_ref):
    n = x.shape[0]
    zeros = jnp.zeros((SUBLANES, x.shape[1]), F32)
    pad_ref[0:SUBLANES, :] = zeros
    pad_ref[SUBLANES + n:, :] = zeros
    pad_ref[SUBLANES:SUBLANES + n, :] = x
    out = None
    for j in range(w.shape[0]):
        term = pad_ref[SUBLANES + j - left:SUBLANES + j - left + n, :] * w[j:j + 1, :]
        out = term if out is None else out + term
    return out


def _inproj_kernel(r_ref, g_ref, w_ref, alog_ref, dtb_ref, proj_ref, pack_ref, packt_ref):
    h = _rms(r_ref[...], g_ref[...])
    res = _mm(h, w_ref[...])
    for j in range(N_MAIN_BLK):
        proj_ref[j] = res[:, j * LANES:(j + 1) * LANES].astype(proj_ref.dtype)
    gates = res[:, MAIN_COLS:]
    tm = gates.shape[0]
    beta = _sigmoid(gates)
    g = -jnp.exp(alog_ref[...]) * jax.nn.softplus(gates + dtb_ref[...])
    sq = (CHUNK, CHUNK)
    tri = jnp.concatenate([(_col_iota(sq) <= _row_iota(sq)).astype(MXU_DTYPE),
                           (_col_iota(sq) >= _row_iota(sq)).astype(MXU_DTYPE)], axis=0)
    lane = _col_iota((CHUNK, LANES))
    for c in range(tm // CHUNK):
        gc = g[c * CHUNK:(c + 1) * CHUNK]
        p1 = gc.astype(MXU_DTYPE)
        r1 = gc - p1.astype(F32)
        p2 = r1.astype(MXU_DTYPE)
        p3 = (r1 - p2.astype(F32)).astype(MXU_DTYPE)
        cs = jnp.dot(tri, jnp.concatenate([p1, p2, p3], axis=1), preferred_element_type=F32)
        cs = cs[:, :LANES] + cs[:, LANES:2 * LANES] + cs[:, 2 * LANES:]
        prefix, suffix = cs[:CHUNK], cs[CHUNK:]
        pk = jnp.where(lane < 8, beta[c * CHUNK:(c + 1) * CHUNK],
                       jnp.where(lane < 12, prefix, suffix))
        for hd in range(DN_HEADS):
            pack_ref[hd, c * CHUNK:(c + 1) * CHUNK, :] = pk if hd == 0 else pltpu.roll(pk, LANES - hd, 1)
        pkt = pk.T
        for hd in range(DN_HEADS):
            packt_ref[hd, :, c * CHUNK:(c + 1) * CHUNK] = jnp.concatenate(
                [pkt[8 + hd:9 + hd], pkt[12 + hd:13 + hd], jnp.zeros((PACKT_ROWS - 2, CHUNK), F32)], axis=0)


def _inproj(r2d, g1, w, alog, dtb):
    t = r2d.shape[0]
    tm = TOKEN_TILE
    return pl.pallas_call(
        _inproj_kernel,
        name="inproj",
        grid=(t // tm,),
        in_specs=[
            pl.BlockSpec((tm, D_MODEL), lambda i: (i, 0)),
            pl.BlockSpec((1, D_MODEL), lambda i: (0, 0)),
            pl.BlockSpec((D_MODEL, PROJ_COLS), lambda i: (0, 0)),
            pl.BlockSpec((1, LANES), lambda i: (0, 0)),
            pl.BlockSpec((1, LANES), lambda i: (0, 0)),
        ],
        out_specs=[
            pl.BlockSpec((N_MAIN_BLK, tm, LANES), lambda i: (0, i, 0)),
            pl.BlockSpec((DN_HEADS, tm, LANES), lambda i: (0, i, 0)),
            pl.BlockSpec((DN_HEADS, PACKT_ROWS, tm), lambda i: (0, 0, i)),
        ],
        out_shape=[
            jax.ShapeDtypeStruct((N_MAIN_BLK, t, LANES), MXU_DTYPE),
            jax.ShapeDtypeStruct((DN_HEADS, t, LANES), F32),
            jax.ShapeDtypeStruct((DN_HEADS, PACKT_ROWS, t), F32),
        ],
        compiler_params=pltpu.CompilerParams(dimension_semantics=("arbitrary",),
                                             vmem_limit_bytes=VMEM_LIMIT),
    )(r2d, g1, w, alog, dtb)


N_TRI_MASKS = 6


def _tri_masks():
    i = np.arange(CHUNK)[:, None]
    j = np.arange(CHUNK)[None, :]
    out = np.zeros((2, N_TRI_MASKS, CHUNK, CHUNK), np.float32)
    for d in range(2):
        hi, lo = (j, i) if d == 1 else (i, j)
        out[d, 0] = (hi // SUBLANES == lo // SUBLANES) & (hi > lo)
        s, lvl = SUBLANES, 1
        while s < CHUNK:
            out[d, lvl] = (hi // (2 * s) == lo // (2 * s)) & ((hi // s) % 2 == 1) & ((lo // s) % 2 == 0)
            s, lvl = 2 * s, lvl + 1
        out[d, N_TRI_MASKS - 1] = hi >= lo
    return out


def _unit_tri_inverse_minus_eye(a_list, mask_of):
    idx = range(len(a_list))
    x = [a_list[i] * mask_of(i, 0) for i in idx]
    y2 = [_mm(x[i], x[i]) for i in idx]
    y4 = [_mm(y2[i], y2[i]) for i in idx]
    xy2 = [_mm(x[i], y2[i]) for i in idx]
    m1 = [y2[i] - x[i].astype(F32) - xy2[i] for i in idx]
    m1y4 = [_mm(m1[i], y4[i]) for i in idx]
    n = [m1[i] + y4[i] + m1y4[i] for i in idx]
    for lvl in range(1, N_TRI_MASKS - 1):
        b = [a_list[i] * mask_of(i, lvl) for i in idx]
        p = [b[i].astype(F32) + _mm(n[i], b[i]) for i in idx]
        pn = [_mm(p[i], n[i]) for i in idx]
        n = [n[i] - p[i] - pn[i] for i in idx]
    return n


def _dn_kernel(q_ref, k_ref, v_ref, z_ref, pack_ref, packt_ref, cwq_ref, cwk_ref, cwv_ref, ng_ref,
               mask_ref, out_ref, pad_ref, qs_ref, ks_ref, vs_ref, c_ref, nmq_ref, cd_ref, o_ref):
    seq = q_ref.shape[1]
    nc = seq // CHUNK

    def l2n(x):
        return x * lax.rsqrt(jnp.sum(x * x, axis=-1, keepdims=True) + EPS)

    def conv_silu(ref, cw_ref):
        y = _seq_conv(ref[0].astype(F32), cw_ref[...], 2, pad_ref)
        return y * _sigmoid(y)

    qs_ref[...] = l2n(conv_silu(q_ref, cwq_ref)) * (DN_HEAD_DIM ** -0.5)
    ks_ref[...] = l2n(conv_silu(k_ref, cwk_ref))
    vs_ref[...] = conv_silu(v_ref, cwv_ref)
    o_ref[...] = jnp.zeros_like(o_ref)

    sq = (CHUNK, CHUNK)
    eye = (_row_iota(sq) == _col_iota(sq)).astype(MXU_DTYPE)

    def phase_a(gi, carry):
        chunks = []
        for c in range(PHASE_A_GROUP):
            n = gi * PHASE_A_GROUP + c
            off = pl.multiple_of(n * CHUNK, CHUNK)
            chunks.append((n, off, qs_ref[pl.ds(off, CHUNK), :], ks_ref[pl.ds(off, CHUNK), :]))
        kqs = [_mm_nt(jnp.concatenate([kc.astype(MXU_DTYPE), qc.astype(MXU_DTYPE), eye], axis=0), kc)
               for (_, _, qc, kc) in chunks]
        inst = []
        for (n, off, qc, kc), kq in zip(chunks, kqs):
            pk = pack_ref[0, pl.ds(off, CHUNK), :]
            vc = vs_ref[pl.ds(off, CHUNK), :]
            for d in range(2):
                beta = jnp.broadcast_to(pk[:, 4 * d:4 * d + 1], sq)
                gc = jnp.broadcast_to(pk[:, 8 + 4 * d:9 + 4 * d], sq)
                gc_row = packt_ref[0, d:d + 1, pl.ds(off, CHUNK)]
                glast = gc[0:1, :] if d == 1 else gc[CHUNK - 1:CHUNK, :]
                e_in = jnp.exp(gc)
                decay = jnp.exp(jnp.minimum(gc - gc_row, 0.0)) * mask_ref[d, N_TRI_MASKS - 1].astype(F32)
                attn = (kq[CHUNK:2 * CHUNK] * decay).astype(MXU_DTYPE)
                kdt = (kq[2 * CHUNK:] * jnp.exp(glast - gc_row)).astype(MXU_DTYPE)
                cd_ref[d, n] = jnp.exp(jnp.broadcast_to(glast, (SUBLANES, LANES)))
                rhs = jnp.concatenate([vc * beta, kc * (beta * e_in)], axis=1).astype(MXU_DTYPE)
                a_raw = (beta * kq[:CHUNK] * decay).astype(MXU_DTYPE)
                inst.append((d, n, off, a_raw, rhs, attn, kdt, qc * e_in))
        idx = range(len(inst))
        n_inv = _unit_tri_inverse_minus_eye([t[3] for t in inst], lambda i, m: mask_ref[inst[i][0], m])
        uw = [(inst[i][4].astype(F32) + _mm(n_inv[i], inst[i][4])).astype(MXU_DTYPE) for i in idx]
        kuw = [_mm(inst[i][6], uw[i]) for i in idx]
        auw = [_mm(inst[i][5], uw[i]) for i in idx]
        for i in idx:
            d, n, off = inst[i][:3]
            c_ref[d, n] = kuw[i][:, :DN_HEAD_DIM]
            nmq_ref[d, n, :CHUNK, :] = (-kuw[i][:, DN_HEAD_DIM:]).astype(nmq_ref.dtype)
            nmq_ref[d, n, CHUNK:, :] = (inst[i][7] - auw[i][:, DN_HEAD_DIM:]).astype(nmq_ref.dtype)
            o_ref[pl.ds(off, CHUNK), :] += auw[i][:, :DN_HEAD_DIM]
        return carry

    lax.fori_loop(0, nc // PHASE_A_GROUP, phase_a, 0)

    def phase_b(i, states):
        ns = (i, nc - 1 - i)
        rs = [_mm(nmq_ref[d, ns[d]], states[d]) for d in range(2)]
        for d in range(2):
            o_ref[pl.ds(pl.multiple_of(ns[d] * CHUNK, CHUNK), CHUNK), :] += rs[d][CHUNK:]
        return tuple(states[d] * cd_ref[d, ns[d]][0:1, :] + rs[d][:CHUNK] + c_ref[d, ns[d]] for d in range(2))

    zero = jnp.zeros((DN_HEAD_DIM, DN_HEAD_DIM), F32)
    lax.fori_loop(0, nc, phase_b, (zero, zero))

    z = z_ref[0].astype(F32)
    out_ref[0] = (_rms(o_ref[...], ng_ref[...]) * (z * _sigmoid(z))).astype(out_ref.dtype)


def _deltanet(proj, pack, packt, conv_w, norm_g, batch, seq):
    nc = seq // CHUNK
    assert nc % PHASE_A_GROUP == 0
    blk = (1, seq, LANES)
    masks = jnp.asarray(_tri_masks(), MXU_DTYPE)

    def col(j0):
        return pl.BlockSpec(blk, lambda b, h: (j0 + h, b, 0))

    def cw(j0):
        return pl.BlockSpec((4, LANES), lambda b, h: (0, j0 + h))

    return pl.pallas_call(
        _dn_kernel,
        name="deltanet",
        grid=(batch, DN_HEADS),
        in_specs=[col(0), col(DN_HEADS), col(2 * DN_HEADS), col(3 * DN_HEADS),
                  pl.BlockSpec(blk, lambda b, h: (h, b, 0)),
                  pl.BlockSpec((1, PACKT_ROWS, seq), lambda b, h: (h, 0, b)),
                  cw(0), cw(DN_HEADS), cw(2 * DN_HEADS),
                  pl.BlockSpec((1, LANES), lambda b, h: (0, 0)),
                  pl.BlockSpec(masks.shape, lambda b, h: (0, 0, 0, 0))],
        out_specs=pl.BlockSpec(blk, lambda b, h: (h, b, 0)),
        out_shape=jax.ShapeDtypeStruct((DN_HEADS, batch * seq, LANES), MXU_DTYPE),
        scratch_shapes=[
            pltpu.VMEM((seq + 2 * SUBLANES, LANES), F32),
            pltpu.VMEM((seq, LANES), F32),
            pltpu.VMEM((seq, LANES), F32),
            pltpu.VMEM((seq, LANES), F32),
            pltpu.VMEM((2, nc, DN_HEAD_DIM, DN_HEAD_DIM), F32),
            pltpu.VMEM((2, nc, 2 * CHUNK, LANES), MXU_DTYPE),
            pltpu.VMEM((2, nc, SUBLANES, LANES), F32),
            pltpu.VMEM((seq, LANES), F32),
        ],
        compiler_params=pltpu.CompilerParams(dimension_semantics=("arbitrary", "arbitrary"),
                                             vmem_limit_bytes=VMEM_LIMIT),
    )(proj, proj, proj, proj, pack, packt, conv_w, conv_w, conv_w, norm_g, masks)


def _scan_level_sizes(seq):
    sizes = [seq]
    while sizes[-1] > SCAN_BOTTOM_ROWS:
        assert sizes[-1] % (SCAN_FANOUT * SUBLANES) == 0
        sizes.append(sizes[-1] // SCAN_FANOUT)
    return sizes


def _lru_kernel(lx_ref, lg_ref, cw_ref, cb_ref, wg_ref, bg_ref, lam_ref, y_ref, pad_ref, h_ref, *lvl):
    nlev = (len(lvl) + 1) // 2
    ab = lvl[:nlev]
    cs = (None,) + tuple(lvl[nlev:])
    seq = lx_ref.shape[1]
    fan = SCAN_FANOUT
    sizes = [r.shape[2] for r in ab]

    xc = _seq_conv(lx_ref[0].astype(F32), cw_ref[...], 2, pad_ref) + cb_ref[...]
    gates = _mm(xc, wg_ref[0]) + bg_ref[0]
    for d in range(2):
        r = _sigmoid(gates[:, (2 * d) * LANES:(2 * d + 1) * LANES])
        ig = _sigmoid(gates[:, (2 * d + 1) * LANES:(2 * d + 2) * LANES])
        log_a = -LRU_C * r * jax.nn.softplus(-lam_ref[0, d:d + 1, :])
        a = jnp.exp(log_a)
        ab[0][d, 0] = a
        ab[0][d, 1] = jnp.sqrt(-jnp.tanh(log_a) * (a * a + 1.0)) * (ig * xc)
        for l in range(1, nlev):
            cs[l][d, 0:SUBLANES, :] = jnp.zeros((SUBLANES, LANES), F32)
            cs[l][d, SUBLANES + sizes[l]:, :] = jnp.zeros((SUBLANES, LANES), F32)

    def slab(l, j):
        return pl.ds(j, sizes[l] // fan, stride=fan)

    for l in range(nlev - 1):
        for d in range(2):
            order = list(range(fan)) if d == 0 else list(range(fan - 1, -1, -1))
            p = ab[l][d, 0, slab(l, order[0]), :]
            h = ab[l][d, 1, slab(l, order[0]), :]
            for j in order[1:]:
                aj = ab[l][d, 0, slab(l, j), :]
                h = aj * h + ab[l][d, 1, slab(l, j), :]
                p = aj * p
                ab[l][d, 0, slab(l, j), :] = p
                ab[l][d, 1, slab(l, j), :] = h
            ab[l + 1][d, 0] = p
            ab[l + 1][d, 1] = h

    top = nlev - 1
    rows = sizes[top]
    row = _row_iota((rows, LANES))
    for d in range(2):
        a = ab[top][d, 0]
        b = ab[top][d, 1]
        s = 1
        while s < rows:
            sh = rows - s if d == 1 else s
            keep = (row < rows - s) if d == 1 else (row >= s)
            b = jnp.where(keep, a * pltpu.roll(b, sh, 0) + b, b)
            a = jnp.where(keep, a * pltpu.roll(a, sh, 0), a)
            s *= 2
        if top == 0:
            ab[0][d, 1] = b
        else:
            cs[top][d, SUBLANES:SUBLANES + rows, :] = b

    for l in range(nlev - 2, -1, -1):
        n = sizes[l] // fan
        carry = [cs[l + 1][d, SUBLANES - 1 + 2 * d:SUBLANES - 1 + 2 * d + n, :] for d in range(2)]
        for j in range(fan):
            true = [ab[l][d, 1, slab(l, j), :] + ab[l][d, 0, slab(l, j), :] * carry[d] for d in range(2)]
            if l == 0:
                h_ref[slab(0, j), :] = true[0] + true[1]
            else:
                for d in range(2):
                    cs[l][d, pl.ds(SUBLANES + j, n, stride=fan), :] = true[d]
    if nlev == 1:
        h_ref[...] = ab[0][0, 1] + ab[0][1, 1]
    y_ref[0] = (jax.nn.gelu(lg_ref[0].astype(F32)) * h_ref[...]).astype(y_ref.dtype)


def _rglru(proj, conv_w, conv_b, wgate, bgate, lam, batch, seq):
    nblk = LRU_WIDTH // LANES
    blk = (1, seq, LANES)
    j_lx = 4 * DN_HEADS
    j_lg = j_lx + nblk
    sizes = _scan_level_sizes(seq)
    return pl.pallas_call(
        _lru_kernel,
        name="rglru",
        grid=(batch, nblk),
        in_specs=[
            pl.BlockSpec(blk, lambda b, c: (j_lx + c, b, 0)),
            pl.BlockSpec(blk, lambda b, c: (j_lg + c, b, 0)),
            pl.BlockSpec((4, LANES), lambda b, c: (0, c)),
            pl.BlockSpec((1, LANES), lambda b, c: (0, c)),
            pl.BlockSpec((1, LANES, 4 * LANES), lambda b, c: (c, 0, 0)),
            pl.BlockSpec((1, 1, 4 * LANES), lambda b, c: (c, 0, 0)),
            pl.BlockSpec((1, 2, LANES), lambda b, c: (c, 0, 0)),
        ],
        out_specs=pl.BlockSpec(blk, lambda b, c: (c, b, 0)),
        out_shape=jax.ShapeDtypeStruct((nblk, batch * seq, LANES), MXU_DTYPE),
        scratch_shapes=(
            [pltpu.VMEM((seq + 2 * SUBLANES, LANES), F32), pltpu.VMEM((seq, LANES), F32)]
            + [pltpu.VMEM((2, 2, m, LANES), F32) for m in sizes]
            + [pltpu.VMEM((2, m + 2 * SUBLANES, LANES), F32) for m in sizes[1:]]),
        compiler_params=pltpu.CompilerParams(dimension_semantics=("arbitrary", "arbitrary"),
                                             vmem_limit_bytes=VMEM_LIMIT),
    )(proj, proj, conv_w, conv_b, wgate, bgate, lam)


def _mixer_kernel(rp_ref, rm_ref, rn_ref, dnp_ref, dnm_ref, dnn_ref, lyp_ref, lym_ref, lyn_ref, p_ref,
                  lng_ref, wo_ref, g2_ref, wg_ref, wu_ref, cw_ref, cb_ref, wd_ref,
                  png_ref, pwg_ref, pbg_ref, pwp_ref, fg_ref, out_ref, mix_ref, hcat_ref, *, final):
    t = pl.program_id(1)
    tm = rm_ref.shape[1]
    halo = BF16_ROWS
    segments = ((0, halo, rp_ref, dnp_ref, lyp_ref), (halo, tm, rm_ref, dnm_ref, lym_ref),
                (halo + tm, halo, rn_ref, dnn_ref, lyn_ref))
    for lo, n, _, dn_ref, ly_ref in segments:
        for j in range(DN_HEADS):
            mix_ref[lo:lo + n, j * LANES:(j + 1) * LANES] = dn_ref[j]
        ly = jnp.concatenate([ly_ref[j] for j in range(ly_ref.shape[0])], axis=1).astype(F32)
        mix_ref[lo:lo + n, DN_WIDTH:] = _rms(ly, lng_ref[...]).astype(mix_ref.dtype)
    proj = jnp.dot(mix_ref[...], wo_ref[...], preferred_element_type=F32)
    r1 = rm_ref[0] + proj[halo:halo + tm]
    hcat_ref[halo:halo + tm, :] = _rms(r1, g2_ref[...]).astype(hcat_ref.dtype)
    h_prev = _rms(rp_ref[0] + proj[:halo], g2_ref[...])
    h_next = _rms(rn_ref[0] + proj[halo + tm:], g2_ref[...])
    hcat_ref[0:halo, :] = jnp.where(t > 0, h_prev, 0.0).astype(hcat_ref.dtype)
    hcat_ref[halo + tm:, :] = jnp.where(t < pl.num_programs(1) - 1, h_next, 0.0).astype(hcat_ref.dtype)

    rows = tm + 2 * halo
    acc = r1
    for c in range(D_FF // FF_CHUNK):
        cs = slice(c * FF_CHUNK, (c + 1) * FF_CHUNK)
        gfull = jnp.dot(hcat_ref[...], wg_ref[:, cs], preferred_element_type=F32)
        gate = (pltpu.roll(gfull, 1, 0)[halo:halo + tm] * cw_ref[0:1, cs]
                + gfull[halo:halo + tm] * cw_ref[1:2, cs]
                + pltpu.roll(gfull, rows - 1, 0)[halo:halo + tm] * cw_ref[2:3, cs]
                + cb_ref[:, cs])
        up = jnp.dot(hcat_ref[halo:halo + tm, :], wu_ref[:, cs], preferred_element_type=F32)
        acc = acc + _mm(jax.nn.gelu(gate) * up, wd_ref[cs, :])
    pgate = _sigmoid(_mm(_rms(acc, png_ref[...]), pwg_ref[...]) + pbg_ref[...])
    res = acc + pgate * _mm(p_ref[0, 0], pwp_ref[...])
    out_ref[0] = _rms(res, fg_ref[...]) if final else res


def _mixer(r, dn, ly, p, layer, final, lng, wo, g2, wg, wu, cw, cb, wd, png, pwg, pbg, pwp, fg):
    batch, seq, _ = r.shape
    tm = TOKEN_TILE
    nt = seq // tm
    hb = tm // BF16_ROWS
    nhb = seq // BF16_ROWS
    nb = LRU_WIDTH // LANES

    def prev_blk(t):
        return jnp.maximum(t * hb - 1, 0)

    def next_blk(t):
        return jnp.minimum((t + 1) * hb, nhb - 1)

    def whole(shape):
        return pl.BlockSpec(shape, lambda b, t: (0,) * len(shape))

    def tok3(nlead):
        return [pl.BlockSpec((nlead, BF16_ROWS, LANES), lambda b, t: (0, b * nhb + prev_blk(t), 0)),
                pl.BlockSpec((nlead, tm, LANES), lambda b, t: (0, b * nt + t, 0)),
                pl.BlockSpec((nlead, BF16_ROWS, LANES), lambda b, t: (0, b * nhb + next_blk(t), 0))]

    return pl.pallas_call(
        functools.partial(_mixer_kernel, final=final),
        name="mixer",
        grid=(batch, nt),
        in_specs=[
            pl.BlockSpec((1, BF16_ROWS, D_MODEL), lambda b, t: (b, prev_blk(t), 0)),
            pl.BlockSpec((1, tm, D_MODEL), lambda b, t: (b, t, 0)),
            pl.BlockSpec((1, BF16_ROWS, D_MODEL), lambda b, t: (b, next_blk(t), 0)),
            *tok3(DN_HEADS), *tok3(nb),
            pl.BlockSpec((1, 1, tm, PLE_DIM), lambda b, t: (layer, b, t, 0)),
            whole((1, LRU_WIDTH)), whole((D_MODEL, D_MODEL)), whole((1, D_MODEL)),
            whole((D_MODEL, D_FF)), whole((D_MODEL, D_FF)), whole((3, D_FF)), whole((1, D_FF)),
            whole((D_FF, D_MODEL)), whole((1, D_MODEL)), whole((D_MODEL, D_MODEL)),
            whole((1, D_MODEL)), whole((PLE_DIM, D_MODEL)), whole((1, D_MODEL)),
        ],
        out_specs=pl.BlockSpec((1, tm, D_MODEL), lambda b, t: (b, t, 0)),
        out_shape=jax.ShapeDtypeStruct((batch, seq, D_MODEL), F32),
        scratch_shapes=[pltpu.VMEM((tm + 2 * BF16_ROWS, D_MODEL), MXU_DTYPE),
                        pltpu.VMEM((tm + 2 * BF16_ROWS, D_MODEL), MXU_DTYPE)],
        compiler_params=pltpu.CompilerParams(dimension_semantics=("arbitrary", "arbitrary"),
                                             vmem_limit_bytes=VMEM_LIMIT),
    )(r, r, r, dn, dn, dn, ly, ly, ly, p, lng, wo, g2, wg, wu, cw, cb, wd, png, pwg, pbg, pwp, fg)


def _row(v):
    return v.reshape(1, -1).astype(F32)


def _gate_lane_row(v):
    return jnp.zeros((1, LANES), F32).at[0, 8:16].set(v.reshape(-1).astype(F32))


def _lru_gate_weights(wa, ba, wx, bx):
    nblk = LRU_WIDTH // LANES
    per = LANES // LRU_BLOCK

    def bd(w):
        w = w.reshape(nblk, per, LRU_BLOCK, LRU_BLOCK)
        eye = jnp.eye(per, dtype=w.dtype)
        return jnp.einsum('npcd,pq->npcqd', w, eye).reshape(nblk, LANES, LANES)

    wcat = jnp.concatenate([bd(wa[0]), bd(wx[0]), bd(wa[1]), bd(wx[1])], axis=2)
    bcat = jnp.concatenate([ba[0].reshape(nblk, 1, LANES), bx[0].reshape(nblk, 1, LANES),
                            ba[1].reshape(nblk, 1, LANES), bx[1].reshape(nblk, 1, LANES)], axis=2)
    return wcat.astype(MXU_DTYPE), bcat.astype(F32)


def kernel(x, p, norm1_g, w_in, dn_conv_w, dn_a_log, dn_dt_bias, dn_norm_g, lru_conv_w, lru_conv_b,
           lru_wa, lru_ba, lru_wx, lru_bx, lru_lambda, lru_norm_g, w_out, norm2_g, ffn_wg, ffn_wu,
           ffn_conv_w, ffn_conv_b, ffn_wd, ple_norm_g, ple_wg, ple_bg, ple_wp, final_g):
    batch, seq, _ = x.shape
    depth = w_in.shape[0]
    assert seq % TOKEN_TILE == 0 and seq % CHUNK == 0 and (batch * seq) % TOKEN_TILE == 0
    nblk = LRU_WIDTH // LANES
    r = x
    for i in range(depth):
        w_perm = jnp.concatenate(
            [w_in[i][:, :BETA_OFF], w_in[i][:, LX_OFF:], w_in[i][:, BETA_OFF:LX_OFF],
             jnp.zeros((D_MODEL, LANES - (LX_OFF - BETA_OFF)), w_in.dtype)], axis=1).astype(MXU_DTYPE)
        proj, pack, packt = _inproj(r.reshape(batch * seq, D_MODEL), _row(norm1_g[i]), w_perm,
                                    _gate_lane_row(dn_a_log[i]), _gate_lane_row(dn_dt_bias[i]))
        dn = _deltanet(proj, pack, packt, dn_conv_w[i], _row(dn_norm_g[i]), batch, seq)
        wgate, bgate = _lru_gate_weights(lru_wa[i], lru_ba[i], lru_wx[i], lru_bx[i])
        lam = jnp.transpose(lru_lambda[i].reshape(2, nblk, LANES), (1, 0, 2)).astype(F32)
        ly = _rglru(proj, lru_conv_w[i], _row(lru_conv_b[i]), wgate, bgate, lam, batch, seq)
        r = _mixer(r, dn, ly, p, i, i == depth - 1, _row(lru_norm_g[i]), w_out[i].astype(MXU_DTYPE),
                   _row(norm2_g[i]), ffn_wg[i].astype(MXU_DTYPE), ffn_wu[i].astype(MXU_DTYPE),
                   ffn_conv_w[i], _row(ffn_conv_b[i]), ffn_wd[i].astype(MXU_DTYPE), _row(ple_norm_g[i]),
                   ple_wg[i].astype(MXU_DTYPE), _row(ple_bg[i]), ple_wp[i].astype(MXU_DTYPE), _row(final_g))
    return r
```

```python
import functools

import jax
import numpy as np
import jax.numpy as jnp
from jax import lax
from jax.experimental import pallas as pl
from jax.experimental.pallas import tpu as pltpu

D_MODEL = 1024
DN_HEADS = 4
DN_HEAD_DIM = 128
DN_WIDTH = DN_HEADS * DN_HEAD_DIM
LRU_WIDTH = 512
LRU_BLOCKS = 8
LRU_BLOCK = LRU_WIDTH // LRU_BLOCKS
LRU_C = 8.0
D_FF = 2816
PLE_DIM = 256
EPS = 1e-6
Z_OFF = 3 * DN_WIDTH
BETA_OFF = 4 * DN_WIDTH
LX_OFF = BETA_OFF + 4 * DN_HEADS
IN_COLS = LX_OFF + 2 * LRU_WIDTH

LANES = 128
SUBLANES = 8
BF16_ROWS = 16
CHUNK = 128
PACKT_ROWS = 8
SCAN_FANOUT = 4
SCAN_BOTTOM_ROWS = 32
PHASE_A_GROUP = 8
MAIN_COLS = 4 * DN_WIDTH + 2 * LRU_WIDTH
N_MAIN_BLK = MAIN_COLS // LANES
PROJ_COLS = MAIN_COLS + LANES
TOKEN_TILE = 512
FF_CHUNK = D_FF // 2
VMEM_LIMIT = 56 * 1024 * 1024

MXU_DTYPE = jnp.bfloat16
F32 = jnp.float32


def _mm(a, b):
    return jnp.dot(a.astype(MXU_DTYPE), b.astype(MXU_DTYPE), preferred_element_type=F32)


def _mm_nt(a, b):
    return lax.dot_general(a.astype(MXU_DTYPE), b.astype(MXU_DTYPE), (((1,), (1,)), ((), ())),
                           preferred_element_type=F32)


def _rms(x, g):
    return x * lax.rsqrt(jnp.mean(x * x, axis=-1, keepdims=True) + EPS) * g


def _row_iota(shape):
    return lax.broadcasted_iota(jnp.int32, shape, 0)


def _col_iota(shape):
    return lax.broadcasted_iota(jnp.int32, shape, 1)


def _sigmoid(x):
    return 0.5 * jnp.tanh(0.5 * x) + 0.5


def _seq_conv(x, w, left, pad_ref):
    n = x.shape[0]
    zeros = jnp.zeros((SUBLANES, x.shape[1]), F32)
    pad_ref[0:SUBLANES, :] = zeros
    pad_ref[SUBLANES + n:, :] = zeros
    pad_ref[SUBLANES:SUBLANES + n, :] = x
    out = None
    for j in range(w.shape[0]):
        term = pad_ref[SUBLANES + j - left:SUBLANES + j - left + n, :] * w[j:j + 1, :]
        out = term if out is None else out + term
    return out


def _inproj_kernel(r_ref, g_ref, w_ref, alog_ref, dtb_ref, proj_ref, pack_ref, packt_ref):
    h = _rms(r_ref[...], g_ref[...])
    res = _mm(h, w_ref[...])
    for j in range(N_MAIN_BLK):
        proj_ref[j] = res[:, j * LANES:(j + 1) * LANES].astype(proj_ref.dtype)
    gates = res[:, MAIN_COLS:]
    tm = gates.shape[0]
    beta = _sigmoid(gates)
    g = -jnp.exp(alog_ref[...]) * jax.nn.softplus(gates + dtb_ref[...])
    sq = (CHUNK, CHUNK)
    tri = jnp.concatenate([(_col_iota(sq) <= _row_iota(sq)).astype(MXU_DTYPE),
                           (_col_iota(sq) >= _row_iota(sq)).astype(MXU_DTYPE)], axis=0)
    lane = _col_iota((CHUNK, LANES))
    for c in range(tm // CHUNK):
        gc = g[c * CHUNK:(c + 1) * CHUNK]
        p1 = gc.astype(MXU_DTYPE)
        r1 = gc - p1.astype(F32)
        p2 = r1.astype(MXU_DTYPE)
        p3 = (r1 - p2.astype(F32)).astype(MXU_DTYPE)
        cs = jnp.dot(tri, jnp.concatenate([p1, p2, p3], axis=1), preferred_element_type=F32)
        cs = cs[:, :LANES] + cs[:, LANES:2 * LANES] + cs[:, 2 * LANES:]
        prefix, suffix = cs[:CHUNK], cs[CHUNK:]
        pk = jnp.where(lane < 8, beta[c * CHUNK:(c + 1) * CHUNK],
                       jnp.where(lane < 12, prefix, suffix))
        for hd in range(DN_HEADS):
            pack_ref[hd, c * CHUNK:(c + 1) * CHUNK, :] = pk if hd == 0 else pltpu.roll(pk, LANES - hd, 1)
        pkt = pk.T
        for hd in range(DN_HEADS):
            packt_ref[hd, :, c * CHUNK:(c + 1) * CHUNK] = jnp.concatenate(
                [pkt[8 + hd:9 + hd], pkt[12 + hd:13 + hd], jnp.zeros((PACKT_ROWS - 2, CHUNK), F32)], axis=0)


def _inproj(r2d, g1, w, alog, dtb):
    t = r2d.shape[0]
    tm = TOKEN_TILE
    return pl.pallas_call(
        _inproj_kernel,
        name="inproj",
        grid=(t // tm,),
        in_specs=[
            pl.BlockSpec((tm, D_MODEL), lambda i: (i, 0)),
            pl.BlockSpec((1, D_MODEL), lambda i: (0, 0)),
            pl.BlockSpec((D_MODEL, PROJ_COLS), lambda i: (0, 0)),
            pl.BlockSpec((1, LANES), lambda i: (0, 0)),
            pl.BlockSpec((1, LANES), lambda i: (0, 0)),
        ],
        out_specs=[
            pl.BlockSpec((N_MAIN_BLK, tm, LANES), lambda i: (0, i, 0)),
            pl.BlockSpec((DN_HEADS, tm, LANES), lambda i: (0, i, 0)),
            pl.BlockSpec((DN_HEADS, PACKT_ROWS, tm), lambda i: (0, 0, i)),
        ],
        out_shape=[
            jax.ShapeDtypeStruct((N_MAIN_BLK, t, LANES), MXU_DTYPE),
            jax.ShapeDtypeStruct((DN_HEADS, t, LANES), F32),
            jax.ShapeDtypeStruct((DN_HEADS, PACKT_ROWS, t), F32),
        ],
        compiler_params=pltpu.CompilerParams(dimension_semantics=("arbitrary",),
                                             vmem_limit_bytes=VMEM_LIMIT),
    )(r2d, g1, w, alog, dtb)


N_TRI_MASKS = 6


def _tri_masks():
    i = np.arange(CHUNK)[:, None]
    j = np.arange(CHUNK)[None, :]
    out = np.zeros((2, N_TRI_MASKS, CHUNK, CHUNK), np.float32)
    for d in range(2):
        hi, lo = (j, i) if d == 1 else (i, j)
        out[d, 0] = (hi // SUBLANES == lo // SUBLANES) & (hi > lo)
        s, lvl = SUBLANES, 1
        while s < CHUNK:
            out[d, lvl] = (hi // (2 * s) == lo // (2 * s)) & ((hi // s) % 2 == 1) & ((lo // s) % 2 == 0)
            s, lvl = 2 * s, lvl + 1
        out[d, N_TRI_MASKS - 1] = hi >= lo
    return out


def _unit_tri_inverse_minus_eye(a_list, mask_of, tick):
    idx = range(len(a_list))
    x = [a_list[i] * mask_of(i, 0) for i in idx]
    y2 = [_mm(x[i], x[i]) for i in idx]
    tick()
    y4 = [_mm(y2[i], y2[i]) for i in idx]
    tick()
    xy2 = [_mm(x[i], y2[i]) for i in idx]
    tick()
    m1 = [y2[i] - x[i].astype(F32) - xy2[i] for i in idx]
    m1y4 = [_mm(m1[i], y4[i]) for i in idx]
    tick()
    n = [m1[i] + y4[i] + m1y4[i] for i in idx]
    for lvl in range(1, N_TRI_MASKS - 1):
        b = [a_list[i] * mask_of(i, lvl) for i in idx]
        p = [b[i].astype(F32) + _mm(n[i], b[i]) for i in idx]
        tick()
        pn = [_mm(p[i], n[i]) for i in idx]
        tick()
        n = [n[i] - p[i] - pn[i] for i in idx]
    return n


def _dn_kernel(q_ref, k_ref, v_ref, pack_ref, packt_ref, cwq_ref, cwk_ref, cwv_ref, z_ref, ng_ref,
               mask_ref, out_ref, pad_ref, qs_ref, ks_ref, vs_ref, c_ref, nmq_ref, cd_ref, o_ref):
    t = pl.program_id(0)
    cur = t % 2
    prv = 1 - cur
    seq = q_ref.shape[1]
    nc = seq // CHUNK

    @pl.when(t == 0)
    def _():
        c_ref[1] = jnp.zeros(c_ref.shape[1:], c_ref.dtype)
        nmq_ref[1] = jnp.zeros(nmq_ref.shape[1:], nmq_ref.dtype)
        cd_ref[1] = jnp.zeros(cd_ref.shape[1:], cd_ref.dtype)
        o_ref[1] = jnp.zeros(o_ref.shape[1:], o_ref.dtype)

    def l2n(x):
        return x * lax.rsqrt(jnp.sum(x * x, axis=-1, keepdims=True) + EPS)

    def conv_silu(ref, cw_ref):
        y = _seq_conv(ref[0].astype(F32), cw_ref[...], 2, pad_ref)
        return y * _sigmoid(y)

    qs_ref[...] = l2n(conv_silu(q_ref, cwq_ref)) * (DN_HEAD_DIM ** -0.5)
    ks_ref[...] = l2n(conv_silu(k_ref, cwk_ref))
    vs_ref[...] = conv_silu(v_ref, cwv_ref)
    o_ref[cur] = jnp.zeros(o_ref.shape[1:], o_ref.dtype)

    sq = (CHUNK, CHUNK)
    eye = (_row_iota(sq) == _col_iota(sq)).astype(MXU_DTYPE)

    def fused(gi, states):
        states = list(states)
        b_done = [0]

        def tick():
            if b_done[0] == PHASE_A_GROUP:
                return
            i = gi * PHASE_A_GROUP + b_done[0]
            b_done[0] += 1
            ns = (i, nc - 1 - i)
            rs = [_mm(nmq_ref[prv, d, ns[d]], states[d]) for d in range(2)]
            for d in range(2):
                o_ref[prv, pl.ds(pl.multiple_of(ns[d] * CHUNK, CHUNK), CHUNK), :] += rs[d][CHUNK:]
                states[d] = states[d] * cd_ref[prv, d, ns[d]][0:1, :] + rs[d][:CHUNK] + c_ref[prv, d, ns[d]]

        chunks = []
        for c in range(PHASE_A_GROUP):
            n = gi * PHASE_A_GROUP + c
            off = pl.multiple_of(n * CHUNK, CHUNK)
            chunks.append((n, off, qs_ref[pl.ds(off, CHUNK), :], ks_ref[pl.ds(off, CHUNK), :]))
        kqs = [_mm_nt(jnp.concatenate([kc.astype(MXU_DTYPE), qc.astype(MXU_DTYPE), eye], axis=0), kc)
               for (_, _, qc, kc) in chunks]
        tick()
        inst = []
        for (n, off, qc, kc), kq in zip(chunks, kqs):
            pk = pack_ref[0, pl.ds(off, CHUNK), :]
            vc = vs_ref[pl.ds(off, CHUNK), :]
            for d in range(2):
                beta = jnp.broadcast_to(pk[:, 4 * d:4 * d + 1], sq)
                gc = jnp.broadcast_to(pk[:, 8 + 4 * d:9 + 4 * d], sq)
                gc_row = packt_ref[0, d:d + 1, pl.ds(off, CHUNK)]
                glast = gc[0:1, :] if d == 1 else gc[CHUNK - 1:CHUNK, :]
                e_in = jnp.exp(gc)
                decay = jnp.exp(jnp.minimum(gc - gc_row, 0.0)) * mask_ref[d, N_TRI_MASKS - 1].astype(F32)
                attn = (kq[CHUNK:2 * CHUNK] * decay).astype(MXU_DTYPE)
                kdt = (kq[2 * CHUNK:] * jnp.exp(glast - gc_row)).astype(MXU_DTYPE)
                cd_ref[cur, d, n] = jnp.exp(jnp.broadcast_to(glast, (SUBLANES, LANES)))
                rhs = jnp.concatenate([vc * beta, kc * (beta * e_in)], axis=1).astype(MXU_DTYPE)
                a_raw = (beta * kq[:CHUNK] * decay).astype(MXU_DTYPE)
                inst.append((d, n, off, a_raw, rhs, attn, kdt, qc * e_in))
        idx = range(len(inst))
        n_inv = _unit_tri_inverse_minus_eye([v[3] for v in inst], lambda i, m: mask_ref[inst[i][0], m], tick)
        assert b_done[0] == PHASE_A_GROUP
        uw = [(inst[i][4].astype(F32) + _mm(n_inv[i], inst[i][4])).astype(MXU_DTYPE) for i in idx]
        kuw = [_mm(inst[i][6], uw[i]) for i in idx]
        auw = [_mm(inst[i][5], uw[i]) for i in idx]
        for i in idx:
            d, n, off = inst[i][:3]
            c_ref[cur, d, n] = kuw[i][:, :DN_HEAD_DIM]
            nmq_ref[cur, d, n, :CHUNK, :] = (-kuw[i][:, DN_HEAD_DIM:]).astype(nmq_ref.dtype)
            nmq_ref[cur, d, n, CHUNK:, :] = (inst[i][7] - auw[i][:, DN_HEAD_DIM:]).astype(nmq_ref.dtype)
            o_ref[cur, pl.ds(off, CHUNK), :] += auw[i][:, :DN_HEAD_DIM]
        return tuple(states)

    zero = jnp.zeros((DN_HEAD_DIM, DN_HEAD_DIM), F32)
    lax.fori_loop(0, nc // PHASE_A_GROUP, fused, (zero, zero))

    z = z_ref[0].astype(F32)
    out_ref[0] = (_rms(o_ref[prv], ng_ref[...]) * (z * _sigmoid(z))).astype(out_ref.dtype)


def _deltanet(proj, pack, packt, conv_w, norm_g, batch, seq):
    nc = seq // CHUNK
    assert nc % PHASE_A_GROUP == 0
    blk = (1, seq, LANES)
    masks = jnp.asarray(_tri_masks(), MXU_DTYPE)
    items = batch * DN_HEADS

    def cur_item(t):
        i = jnp.minimum(t, items - 1)
        return i % DN_HEADS, i // DN_HEADS

    def prev_item(t):
        i = jnp.maximum(t - 1, 0)
        return i % DN_HEADS, i // DN_HEADS

    def col(j0):
        return pl.BlockSpec(blk, lambda t: (j0 + cur_item(t)[0], cur_item(t)[1], 0))

    def cw(j0):
        return pl.BlockSpec((4, LANES), lambda t: (0, j0 + cur_item(t)[0]))

    return pl.pallas_call(
        _dn_kernel,
        name="deltanet",
        grid=(items + 1,),
        in_specs=[col(0), col(DN_HEADS), col(2 * DN_HEADS),
                  pl.BlockSpec(blk, lambda t: (cur_item(t)[0], cur_item(t)[1], 0)),
                  pl.BlockSpec((1, PACKT_ROWS, seq), lambda t: (cur_item(t)[0], 0, cur_item(t)[1])),
                  cw(0), cw(DN_HEADS), cw(2 * DN_HEADS),
                  pl.BlockSpec(blk, lambda t: (3 * DN_HEADS + prev_item(t)[0], prev_item(t)[1], 0)),
                  pl.BlockSpec((1, LANES), lambda t: (0, 0)),
                  pl.BlockSpec(masks.shape, lambda t: (0, 0, 0, 0))],
        out_specs=pl.BlockSpec(blk, lambda t: (prev_item(t)[0], prev_item(t)[1], 0)),
        out_shape=jax.ShapeDtypeStruct((DN_HEADS, batch * seq, LANES), MXU_DTYPE),
        scratch_shapes=[
            pltpu.VMEM((seq + 2 * SUBLANES, LANES), F32),
            pltpu.VMEM((seq, LANES), F32),
            pltpu.VMEM((seq, LANES), F32),
            pltpu.VMEM((seq, LANES), F32),
            pltpu.VMEM((2, 2, nc, DN_HEAD_DIM, DN_HEAD_DIM), F32),
            pltpu.VMEM((2, 2, nc, 2 * CHUNK, LANES), MXU_DTYPE),
            pltpu.VMEM((2, 2, nc, SUBLANES, LANES), F32),
            pltpu.VMEM((2, seq, LANES), F32),
        ],
        compiler_params=pltpu.CompilerParams(dimension_semantics=("arbitrary",),
                                             vmem_limit_bytes=VMEM_LIMIT),
    )(proj, proj, proj, pack, packt, conv_w, conv_w, conv_w, proj, norm_g, masks)


def _scan_level_sizes(seq):
    sizes = [seq]
    while sizes[-1] > SCAN_BOTTOM_ROWS:
        assert sizes[-1] % (SCAN_FANOUT * SUBLANES) == 0
        sizes.append(sizes[-1] // SCAN_FANOUT)
    return sizes


def _lru_kernel(lx_ref, lg_ref, cw_ref, cb_ref, wg_ref, bg_ref, lam_ref, y_ref, pad_ref, h_ref, *lvl):
    nlev = (len(lvl) + 1) // 2
    ab = lvl[:nlev]
    cs = (None,) + tuple(lvl[nlev:])
    seq = lx_ref.shape[1]
    fan = SCAN_FANOUT
    sizes = [r.shape[2] for r in ab]

    xc = _seq_conv(lx_ref[0].astype(F32), cw_ref[...], 2, pad_ref) + cb_ref[...]
    gates = _mm(xc, wg_ref[0]) + bg_ref[0]
    for d in range(2):
        r = _sigmoid(gates[:, (2 * d) * LANES:(2 * d + 1) * LANES])
        ig = _sigmoid(gates[:, (2 * d + 1) * LANES:(2 * d + 2) * LANES])
        log_a = -LRU_C * r * jax.nn.softplus(-lam_ref[0, d:d + 1, :])
        a = jnp.exp(log_a)
        ab[0][d, 0] = a
        v = -jnp.tanh(log_a) * (a * a + 1.0)
        ab[0][d, 1] = jnp.where(v > 0.0, v * lax.rsqrt(v), 0.0) * (ig * xc)
        for l in range(1, nlev):
            cs[l][d, 0:SUBLANES, :] = jnp.zeros((SUBLANES, LANES), F32)
            cs[l][d, SUBLANES + sizes[l]:, :] = jnp.zeros((SUBLANES, LANES), F32)

    def slab(l, j):
        return pl.ds(j, sizes[l] // fan, stride=fan)

    for l in range(nlev - 1):
        for d in range(2):
            order = list(range(fan)) if d == 0 else list(range(fan - 1, -1, -1))
            p = ab[l][d, 0, slab(l, order[0]), :]
            h = ab[l][d, 1, slab(l, order[0]), :]
            for j in order[1:]:
                aj = ab[l][d, 0, slab(l, j), :]
                h = aj * h + ab[l][d, 1, slab(l, j), :]
                p = aj * p
                ab[l][d, 0, slab(l, j), :] = p
                ab[l][d, 1, slab(l, j), :] = h
            ab[l + 1][d, 0] = p
            ab[l + 1][d, 1] = h

    top = nlev - 1
    rows = sizes[top]
    row = _row_iota((rows, LANES))
    for d in range(2):
        a = ab[top][d, 0]
        b = ab[top][d, 1]
        s = 1
        while s < rows:
            sh = rows - s if d == 1 else s
            keep = (row < rows - s) if d == 1 else (row >= s)
            b = jnp.where(keep, a * pltpu.roll(b, sh, 0) + b, b)
            a = jnp.where(keep, a * pltpu.roll(a, sh, 0), a)
            s *= 2
        if top == 0:
            ab[0][d, 1] = b
        else:
            cs[top][d, SUBLANES:SUBLANES + rows, :] = b

    for l in range(nlev - 2, -1, -1):
        n = sizes[l] // fan
        carry = [cs[l + 1][d, SUBLANES - 1 + 2 * d:SUBLANES - 1 + 2 * d + n, :] for d in range(2)]
        for j in range(fan):
            true = [ab[l][d, 1, slab(l, j), :] + ab[l][d, 0, slab(l, j), :] * carry[d] for d in range(2)]
            if l == 0:
                h_ref[slab(0, j), :] = true[0] + true[1]
            else:
                for d in range(2):
                    cs[l][d, pl.ds(SUBLANES + j, n, stride=fan), :] = true[d]
    if nlev == 1:
        h_ref[...] = ab[0][0, 1] + ab[0][1, 1]
    y_ref[0] = (jax.nn.gelu(lg_ref[0].astype(F32)) * h_ref[...]).astype(y_ref.dtype)


def _rglru(proj, conv_w, conv_b, wgate, bgate, lam, batch, seq):
    nblk = LRU_WIDTH // LANES
    blk = (1, seq, LANES)
    j_lx = 4 * DN_HEADS
    j_lg = j_lx + nblk
    sizes = _scan_level_sizes(seq)
    return pl.pallas_call(
        _lru_kernel,
        name="rglru",
        grid=(batch, nblk),
        in_specs=[
            pl.BlockSpec(blk, lambda b, c: (j_lx + c, b, 0)),
            pl.BlockSpec(blk, lambda b, c: (j_lg + c, b, 0)),
            pl.BlockSpec((4, LANES), lambda b, c: (0, c)),
            pl.BlockSpec((1, LANES), lambda b, c: (0, c)),
            pl.BlockSpec((1, LANES, 4 * LANES), lambda b, c: (c, 0, 0)),
            pl.BlockSpec((1, 1, 4 * LANES), lambda b, c: (c, 0, 0)),
            pl.BlockSpec((1, 2, LANES), lambda b, c: (c, 0, 0)),
        ],
        out_specs=pl.BlockSpec(blk, lambda b, c: (c, b, 0)),
        out_shape=jax.ShapeDtypeStruct((nblk, batch * seq, LANES), MXU_DTYPE),
        scratch_shapes=(
            [pltpu.VMEM((seq + 2 * SUBLANES, LANES), F32), pltpu.VMEM((seq, LANES), F32)]
            + [pltpu.VMEM((2, 2, m, LANES), F32) for m in sizes]
            + [pltpu.VMEM((2, m + 2 * SUBLANES, LANES), F32) for m in sizes[1:]]),
        compiler_params=pltpu.CompilerParams(dimension_semantics=("arbitrary", "arbitrary"),
                                             vmem_limit_bytes=VMEM_LIMIT),
    )(proj, proj, conv_w, conv_b, wgate, bgate, lam)


def _mixer_kernel(rp_ref, rm_ref, rn_ref, dnp_ref, dnm_ref, dnn_ref, lyp_ref, lym_ref, lyn_ref, p_ref,
                  lng_ref, wo_ref, g2_ref, wg_ref, wu_ref, cw_ref, cb_ref, wd_ref,
                  png_ref, pwg_ref, pbg_ref, pwp_ref, fg_ref, out_ref, mix_ref, hcat_ref, *, final):
    t = pl.program_id(1)
    tm = rm_ref.shape[1]
    halo = BF16_ROWS
    segments = ((0, halo, rp_ref, dnp_ref, lyp_ref), (halo, tm, rm_ref, dnm_ref, lym_ref),
                (halo + tm, halo, rn_ref, dnn_ref, lyn_ref))
    for lo, n, _, dn_ref, ly_ref in segments:
        for j in range(DN_HEADS):
            mix_ref[lo:lo + n, j * LANES:(j + 1) * LANES] = dn_ref[j]
        ly = jnp.concatenate([ly_ref[j] for j in range(ly_ref.shape[0])], axis=1).astype(F32)
        mix_ref[lo:lo + n, DN_WIDTH:] = _rms(ly, lng_ref[...]).astype(mix_ref.dtype)
    proj = jnp.dot(mix_ref[...], wo_ref[...], preferred_element_type=F32)
    r1 = rm_ref[0] + proj[halo:halo + tm]
    hcat_ref[halo:halo + tm, :] = _rms(r1, g2_ref[...]).astype(hcat_ref.dtype)
    h_prev = _rms(rp_ref[0] + proj[:halo], g2_ref[...])
    h_next = _rms(rn_ref[0] + proj[halo + tm:], g2_ref[...])
    hcat_ref[0:halo, :] = jnp.where(t > 0, h_prev, 0.0).astype(hcat_ref.dtype)
    hcat_ref[halo + tm:, :] = jnp.where(t < pl.num_programs(1) - 1, h_next, 0.0).astype(hcat_ref.dtype)

    rows = tm + 2 * halo
    acc = r1
    for c in range(D_FF // FF_CHUNK):
        cs = slice(c * FF_CHUNK, (c + 1) * FF_CHUNK)
        gfull = jnp.dot(hcat_ref[...], wg_ref[:, cs], preferred_element_type=F32)
        gate = (pltpu.roll(gfull, 1, 0)[halo:halo + tm] * cw_ref[0:1, cs]
                + gfull[halo:halo + tm] * cw_ref[1:2, cs]
                + pltpu.roll(gfull, rows - 1, 0)[halo:halo + tm] * cw_ref[2:3, cs]
                + cb_ref[:, cs])
        up = jnp.dot(hcat_ref[halo:halo + tm, :], wu_ref[:, cs], preferred_element_type=F32)
        acc = acc + _mm(jax.nn.gelu(gate) * up, wd_ref[cs, :])
    pgate = _sigmoid(_mm(_rms(acc, png_ref[...]), pwg_ref[...]) + pbg_ref[...])
    res = acc + pgate * _mm(p_ref[0, 0], pwp_ref[...])
    out_ref[0] = _rms(res, fg_ref[...]) if final else res


def _mixer(r, dn, ly, p, layer, final, lng, wo, g2, wg, wu, cw, cb, wd, png, pwg, pbg, pwp, fg):
    batch, seq, _ = r.shape
    tm = TOKEN_TILE
    nt = seq // tm
    hb = tm // BF16_ROWS
    nhb = seq // BF16_ROWS
    nb = LRU_WIDTH // LANES

    def prev_blk(t):
        return jnp.maximum(t * hb - 1, 0)

    def next_blk(t):
        return jnp.minimum((t + 1) * hb, nhb - 1)

    def whole(shape):
        return pl.BlockSpec(shape, lambda b, t: (0,) * len(shape))

    def tok3(nlead):
        return [pl.BlockSpec((nlead, BF16_ROWS, LANES), lambda b, t: (0, b * nhb + prev_blk(t), 0)),
                pl.BlockSpec((nlead, tm, LANES), lambda b, t: (0, b * nt + t, 0)),
                pl.BlockSpec((nlead, BF16_ROWS, LANES), lambda b, t: (0, b * nhb + next_blk(t), 0))]

    return pl.pallas_call(
        functools.partial(_mixer_kernel, final=final),
        name="mixer",
        grid=(batch, nt),
        in_specs=[
            pl.BlockSpec((1, BF16_ROWS, D_MODEL), lambda b, t: (b, prev_blk(t), 0)),
            pl.BlockSpec((1, tm, D_MODEL), lambda b, t: (b, t, 0)),
            pl.BlockSpec((1, BF16_ROWS, D_MODEL), lambda b, t: (b, next_blk(t), 0)),
            *tok3(DN_HEADS), *tok3(nb),
            pl.BlockSpec((1, 1, tm, PLE_DIM), lambda b, t: (layer, b, t, 0)),
            whole((1, LRU_WIDTH)), whole((D_MODEL, D_MODEL)), whole((1, D_MODEL)),
            whole((D_MODEL, D_FF)), whole((D_MODEL, D_FF)), whole((3, D_FF)), whole((1, D_FF)),
            whole((D_FF, D_MODEL)), whole((1, D_MODEL)), whole((D_MODEL, D_MODEL)),
            whole((1, D_MODEL)), whole((PLE_DIM, D_MODEL)), whole((1, D_MODEL)),
        ],
        out_specs=pl.BlockSpec((1, tm, D_MODEL), lambda b, t: (b, t, 0)),
        out_shape=jax.ShapeDtypeStruct((batch, seq, D_MODEL), F32),
        scratch_shapes=[pltpu.VMEM((tm + 2 * BF16_ROWS, D_MODEL), MXU_DTYPE),
                        pltpu.VMEM((tm + 2 * BF16_ROWS, D_MODEL), MXU_DTYPE)],
        compiler_params=pltpu.CompilerParams(dimension_semantics=("arbitrary", "arbitrary"),
                                             vmem_limit_bytes=VMEM_LIMIT),
    )(r, r, r, dn, dn, dn, ly, ly, ly, p, lng, wo, g2, wg, wu, cw, cb, wd, png, pwg, pbg, pwp, fg)


def _row(v):
    return v.reshape(1, -1).astype(F32)


def _gate_lane_row(v):
    return jnp.zeros((1, LANES), F32).at[0, 8:16].set(v.reshape(-1).astype(F32))


def _lru_gate_weights(wa, ba, wx, bx):
    nblk = LRU_WIDTH // LANES
    per = LANES // LRU_BLOCK

    def bd(w):
        w = w.reshape(nblk, per, LRU_BLOCK, LRU_BLOCK)
        eye = jnp.eye(per, dtype=w.dtype)
        return jnp.einsum('npcd,pq->npcqd', w, eye).reshape(nblk, LANES, LANES)

    wcat = jnp.concatenate([bd(wa[0]), bd(wx[0]), bd(wa[1]), bd(wx[1])], axis=2)
    bcat = jnp.concatenate([ba[0].reshape(nblk, 1, LANES), bx[0].reshape(nblk, 1, LANES),
                            ba[1].reshape(nblk, 1, LANES), bx[1].reshape(nblk, 1, LANES)], axis=2)
    return wcat.astype(MXU_DTYPE), bcat.astype(F32)


def kernel(x, p, norm1_g, w_in, dn_conv_w, dn_a_log, dn_dt_bias, dn_norm_g, lru_conv_w, lru_conv_b,
           lru_wa, lru_ba, lru_wx, lru_bx, lru_lambda, lru_norm_g, w_out, norm2_g, ffn_wg, ffn_wu,
           ffn_conv_w, ffn_conv_b, ffn_wd, ple_norm_g, ple_wg, ple_bg, ple_wp, final_g):
    batch, seq, _ = x.shape
    depth = w_in.shape[0]
    assert seq % TOKEN_TILE == 0 and seq % CHUNK == 0 and (batch * seq) % TOKEN_TILE == 0
    nblk = LRU_WIDTH // LANES
    r = x
    for i in range(depth):
        w_perm = jnp.concatenate(
            [w_in[i][:, :BETA_OFF], w_in[i][:, LX_OFF:], w_in[i][:, BETA_OFF:LX_OFF],
             jnp.zeros((D_MODEL, LANES - (LX_OFF - BETA_OFF)), w_in.dtype)], axis=1).astype(MXU_DTYPE)
        proj, pack, packt = _inproj(r.reshape(batch * seq, D_MODEL), _row(norm1_g[i]), w_perm,
                                    _gate_lane_row(dn_a_log[i]), _gate_lane_row(dn_dt_bias[i]))
        dn = _deltanet(proj, pack, packt, dn_conv_w[i], _row(dn_norm_g[i]), batch, seq)
        wgate, bgate = _lru_gate_weights(lru_wa[i], lru_ba[i], lru_wx[i], lru_bx[i])
        lam = jnp.transpose(lru_lambda[i].reshape(2, nblk, LANES), (1, 0, 2)).astype(F32)
        ly = _rglru(proj, lru_conv_w[i], _row(lru_conv_b[i]), wgate, bgate, lam, batch, seq)
        r = _mixer(r, dn, ly, p, i, i == depth - 1, _row(lru_norm_g[i]), w_out[i].astype(MXU_DTYPE),
                   _row(norm2_g[i]), ffn_wg[i].astype(MXU_DTYPE), ffn_wu[i].astype(MXU_DTYPE),
                   ffn_conv_w[i], _row(ffn_conv_b[i]), ffn_wd[i].astype(MXU_DTYPE), _row(ple_norm_g[i]),
                   ple_wg[i].astype(MXU_DTYPE), _row(ple_bg[i]), ple_wp[i].astype(MXU_DTYPE), _row(final_g))
    return r
```

```python
import functools

import jax
import numpy as np
import jax.numpy as jnp
from jax import lax
from jax.experimental import pallas as pl
from jax.experimental.pallas import tpu as pltpu

D_MODEL = 1024
DN_HEADS = 4
DN_HEAD_DIM = 128
DN_WIDTH = DN_HEADS * DN_HEAD_DIM
LRU_WIDTH = 512
LRU_BLOCKS = 8
LRU_BLOCK = LRU_WIDTH // LRU_BLOCKS
LRU_C = 8.0
D_FF = 2816
PLE_DIM = 256
EPS = 1e-6
Z_OFF = 3 * DN_WIDTH
BETA_OFF = 4 * DN_WIDTH
LX_OFF = BETA_OFF + 4 * DN_HEADS
IN_COLS = LX_OFF + 2 * LRU_WIDTH

LANES = 128
SUBLANES = 8
BF16_ROWS = 16
CHUNK = 128
PACKT_ROWS = 8
SCAN_FANOUT = 4
SCAN_BOTTOM_ROWS = 32
PHASE_A_GROUP = 8
MAIN_COLS = 4 * DN_WIDTH + 2 * LRU_WIDTH
N_MAIN_BLK = MAIN_COLS // LANES
PROJ_COLS = MAIN_COLS + LANES
TOKEN_TILE = 512
INPROJ_TILE = 2 * TOKEN_TILE
MXU_TILE = 256
FF_CHUNKS = (6 * MXU_TILE, 5 * MXU_TILE)
assert sum(FF_CHUNKS) == D_FF
VMEM_LIMIT = 56 * 1024 * 1024

MXU_DTYPE = jnp.bfloat16
F32 = jnp.float32


def _mm(a, b):
    return jnp.dot(a.astype(MXU_DTYPE), b.astype(MXU_DTYPE), preferred_element_type=F32)


def _mm_nt(a, b):
    return lax.dot_general(a.astype(MXU_DTYPE), b.astype(MXU_DTYPE), (((1,), (1,)), ((), ())),
                           preferred_element_type=F32)


def _rms(x, g):
    return x * lax.rsqrt(jnp.mean(x * x, axis=-1, keepdims=True) + EPS) * g


def _row_iota(shape):
    return lax.broadcasted_iota(jnp.int32, shape, 0)


def _col_iota(shape):
    return lax.broadcasted_iota(jnp.int32, shape, 1)


def _sigmoid(x):
    return 0.5 * jnp.tanh(0.5 * x) + 0.5


def _seq_conv(x, w, left, pad_ref):
    n = x.shape[0]
    zeros = jnp.zeros((SUBLANES, x.shape[1]), F32)
    pad_ref[0:SUBLANES, :] = zeros
    pad_ref[SUBLANES + n:, :] = zeros
    pad_ref[SUBLANES:SUBLANES + n, :] = x
    out = None
    for j in range(w.shape[0]):
        term = pad_ref[SUBLANES + j - left:SUBLANES + j - left + n, :] * w[j:j + 1, :]
        out = term if out is None else out + term
    return out


def _inproj_kernel(r_ref, g_ref, w_ref, alog_ref, dtb_ref, proj_ref, pack_ref, packt_ref):
    sq = (CHUNK, CHUNK)
    tri = jnp.concatenate([(_col_iota(sq) <= _row_iota(sq)).astype(MXU_DTYPE),
                           (_col_iota(sq) >= _row_iota(sq)).astype(MXU_DTYPE)], axis=0)
    lane = _col_iota((CHUNK, LANES))
    for s0 in range(0, r_ref.shape[0], TOKEN_TILE):
        h = _rms(r_ref[s0:s0 + TOKEN_TILE, :], g_ref[...])
        res = _mm(h, w_ref[...])
        for j in range(N_MAIN_BLK):
            proj_ref[j, s0:s0 + TOKEN_TILE, :] = res[:, j * LANES:(j + 1) * LANES].astype(proj_ref.dtype)
        gates = res[:, MAIN_COLS:]
        beta = _sigmoid(gates)
        g = -jnp.exp(alog_ref[...]) * jax.nn.softplus(gates + dtb_ref[...])
        for c in range(TOKEN_TILE // CHUNK):
            rows = slice(s0 + c * CHUNK, s0 + (c + 1) * CHUNK)
            gc = g[c * CHUNK:(c + 1) * CHUNK]
            p1 = gc.astype(MXU_DTYPE)
            r1 = gc - p1.astype(F32)
            p2 = r1.astype(MXU_DTYPE)
            p3 = (r1 - p2.astype(F32)).astype(MXU_DTYPE)
            cs = jnp.dot(tri, jnp.concatenate([p1, p2, p3], axis=1), preferred_element_type=F32)
            cs = cs[:, :LANES] + cs[:, LANES:2 * LANES] + cs[:, 2 * LANES:]
            prefix, suffix = cs[:CHUNK], cs[CHUNK:]
            pk = jnp.where(lane < 8, beta[c * CHUNK:(c + 1) * CHUNK],
                           jnp.where(lane < 12, prefix, suffix))
            for hd in range(DN_HEADS):
                pack_ref[hd, rows, :] = pk if hd == 0 else pltpu.roll(pk, LANES - hd, 1)
            pkt = pk.T
            for hd in range(DN_HEADS):
                packt_ref[hd, :, rows] = jnp.concatenate(
                    [pkt[8 + hd:9 + hd], pkt[12 + hd:13 + hd], jnp.zeros((PACKT_ROWS - 2, CHUNK), F32)], axis=0)


def _inproj(r2d, g1, w, alog, dtb):
    t = r2d.shape[0]
    tm = INPROJ_TILE
    return pl.pallas_call(
        _inproj_kernel,
        name="inproj",
        grid=(t // tm,),
        in_specs=[
            pl.BlockSpec((tm, D_MODEL), lambda i: (i, 0)),
            pl.BlockSpec((1, D_MODEL), lambda i: (0, 0)),
            pl.BlockSpec((D_MODEL, PROJ_COLS), lambda i: (0, 0)),
            pl.BlockSpec((1, LANES), lambda i: (0, 0)),
            pl.BlockSpec((1, LANES), lambda i: (0, 0)),
        ],
        out_specs=[
            pl.BlockSpec((N_MAIN_BLK, tm, LANES), lambda i: (0, i, 0)),
            pl.BlockSpec((DN_HEADS, tm, LANES), lambda i: (0, i, 0)),
            pl.BlockSpec((DN_HEADS, PACKT_ROWS, tm), lambda i: (0, 0, i)),
        ],
        out_shape=[
            jax.ShapeDtypeStruct((N_MAIN_BLK, t, LANES), MXU_DTYPE),
            jax.ShapeDtypeStruct((DN_HEADS, t, LANES), F32),
            jax.ShapeDtypeStruct((DN_HEADS, PACKT_ROWS, t), F32),
        ],
        compiler_params=pltpu.CompilerParams(dimension_semantics=("arbitrary",),
                                             vmem_limit_bytes=VMEM_LIMIT),
    )(r2d, g1, w, alog, dtb)


N_TRI_MASKS = 6


def _tri_masks():
    i = np.arange(CHUNK)[:, None]
    j = np.arange(CHUNK)[None, :]
    out = np.zeros((2, N_TRI_MASKS, CHUNK, CHUNK), np.float32)
    for d in range(2):
        hi, lo = (j, i) if d == 1 else (i, j)
        out[d, 0] = (hi // SUBLANES == lo // SUBLANES) & (hi > lo)
        s, lvl = SUBLANES, 1
        while s < CHUNK:
            out[d, lvl] = (hi // (2 * s) == lo // (2 * s)) & ((hi // s) % 2 == 1) & ((lo // s) % 2 == 0)
            s, lvl = 2 * s, lvl + 1
        out[d, N_TRI_MASKS - 1] = hi >= lo
    return out


def _unit_tri_inverse_minus_eye(a_list, mask_of, tick):
    idx = range(len(a_list))
    x = [a_list[i] * mask_of(i, 0) for i in idx]
    y2 = [_mm(x[i], x[i]) for i in idx]
    tick()
    y4 = [_mm(y2[i], y2[i]) for i in idx]
    tick()
    xy2 = [_mm(x[i], y2[i]) for i in idx]
    tick()
    m1 = [y2[i] - x[i].astype(F32) - xy2[i] for i in idx]
    m1y4 = [_mm(m1[i], y4[i]) for i in idx]
    tick()
    n = [m1[i] + y4[i] + m1y4[i] for i in idx]
    for lvl in range(1, N_TRI_MASKS - 1):
        b = [a_list[i] * mask_of(i, lvl) for i in idx]
        p = [b[i].astype(F32) + _mm(n[i], b[i]) for i in idx]
        tick()
        pn = [_mm(p[i], n[i]) for i in idx]
        tick()
        n = [n[i] - p[i] - pn[i] for i in idx]
    return n


def _dn_kernel(q_ref, k_ref, v_ref, pack_ref, packt_ref, cwq_ref, cwk_ref, cwv_ref, z_ref, ng_ref,
               mask_ref, out_ref, pad_ref, qs_ref, ks_ref, vs_ref, c_ref, nmq_ref, cd_ref, o_ref):
    t = pl.program_id(0)
    cur = t % 2
    prv = 1 - cur
    seq = q_ref.shape[1]
    nc = seq // CHUNK

    @pl.when(t == 0)
    def _():
        c_ref[1] = jnp.zeros(c_ref.shape[1:], c_ref.dtype)
        nmq_ref[1] = jnp.zeros(nmq_ref.shape[1:], nmq_ref.dtype)
        cd_ref[1] = jnp.zeros(cd_ref.shape[1:], cd_ref.dtype)
        o_ref[1] = jnp.zeros(o_ref.shape[1:], o_ref.dtype)

    def l2n(x):
        return x * lax.rsqrt(jnp.sum(x * x, axis=-1, keepdims=True) + EPS)

    def conv_silu(ref, cw_ref):
        y = _seq_conv(ref[0].astype(F32), cw_ref[...], 2, pad_ref)
        return y * _sigmoid(y)

    qs_ref[...] = l2n(conv_silu(q_ref, cwq_ref)) * (DN_HEAD_DIM ** -0.5)
    ks_ref[...] = l2n(conv_silu(k_ref, cwk_ref))
    vs_ref[...] = conv_silu(v_ref, cwv_ref)
    o_ref[cur] = jnp.zeros(o_ref.shape[1:], o_ref.dtype)

    sq = (CHUNK, CHUNK)
    eye = (_row_iota(sq) == _col_iota(sq)).astype(MXU_DTYPE)

    def fused(gi, states):
        states = list(states)
        b_done = [0]

        def tick():
            if b_done[0] == PHASE_A_GROUP:
                return
            i = gi * PHASE_A_GROUP + b_done[0]
            b_done[0] += 1
            ns = (i, nc - 1 - i)
            rs = [_mm(nmq_ref[prv, d, ns[d]], states[d]) for d in range(2)]
            for d in range(2):
                o_ref[prv, pl.ds(pl.multiple_of(ns[d] * CHUNK, CHUNK), CHUNK), :] += rs[d][CHUNK:]
                states[d] = states[d] * cd_ref[prv, d, ns[d]][0:1, :] + rs[d][:CHUNK] + c_ref[prv, d, ns[d]]

        chunks = []
        for c in range(PHASE_A_GROUP):
            n = gi * PHASE_A_GROUP + c
            off = pl.multiple_of(n * CHUNK, CHUNK)
            chunks.append((n, off, qs_ref[pl.ds(off, CHUNK), :], ks_ref[pl.ds(off, CHUNK), :]))
        kqs = [_mm_nt(jnp.concatenate([kc.astype(MXU_DTYPE), qc.astype(MXU_DTYPE), eye], axis=0), kc)
               for (_, _, qc, kc) in chunks]
        tick()
        inst = []
        for (n, off, qc, kc), kq in zip(chunks, kqs):
            pk = pack_ref[0, pl.ds(off, CHUNK), :]
            vc = vs_ref[pl.ds(off, CHUNK), :]
            for d in range(2):
                beta = jnp.broadcast_to(pk[:, 4 * d:4 * d + 1], sq)
                gc = jnp.broadcast_to(pk[:, 8 + 4 * d:9 + 4 * d], sq)
                gc_row = packt_ref[0, d:d + 1, pl.ds(off, CHUNK)]
                glast = gc[0:1, :] if d == 1 else gc[CHUNK - 1:CHUNK, :]
                e_in = jnp.exp(gc)
                decay = jnp.exp(jnp.minimum(gc - gc_row, 0.0)) * mask_ref[d, N_TRI_MASKS - 1].astype(F32)
                attn = (kq[CHUNK:2 * CHUNK] * decay).astype(MXU_DTYPE)
                kdt = (kq[2 * CHUNK:] * jnp.exp(glast - gc_row)).astype(MXU_DTYPE)
                cd_ref[cur, d, n] = jnp.exp(jnp.broadcast_to(glast, (SUBLANES, LANES)))
                rhs = jnp.concatenate([vc * beta, kc * (beta * e_in)], axis=1).astype(MXU_DTYPE)
                a_raw = (beta * kq[:CHUNK] * decay).astype(MXU_DTYPE)
                inst.append((d, n, off, a_raw, rhs, attn, kdt, qc * e_in))
        idx = range(len(inst))
        n_inv = _unit_tri_inverse_minus_eye([v[3] for v in inst], lambda i, m: mask_ref[inst[i][0], m], tick)
        assert b_done[0] == PHASE_A_GROUP
        uw = [(inst[i][4].astype(F32) + _mm(n_inv[i], inst[i][4])).astype(MXU_DTYPE) for i in idx]
        kuw = [_mm(inst[i][6], uw[i]) for i in idx]
        auw = [_mm(inst[i][5], uw[i]) for i in idx]
        for i in idx:
            d, n, off = inst[i][:3]
            c_ref[cur, d, n] = kuw[i][:, :DN_HEAD_DIM]
            nmq_ref[cur, d, n, :CHUNK, :] = (-kuw[i][:, DN_HEAD_DIM:]).astype(nmq_ref.dtype)
            nmq_ref[cur, d, n, CHUNK:, :] = (inst[i][7] - auw[i][:, DN_HEAD_DIM:]).astype(nmq_ref.dtype)
            o_ref[cur, pl.ds(off, CHUNK), :] += auw[i][:, :DN_HEAD_DIM]
        return tuple(states)

    zero = jnp.zeros((DN_HEAD_DIM, DN_HEAD_DIM), F32)
    lax.fori_loop(0, nc // PHASE_A_GROUP, fused, (zero, zero))

    z = z_ref[0].astype(F32)
    out_ref[0] = (_rms(o_ref[prv], ng_ref[...]) * (z * _sigmoid(z))).astype(out_ref.dtype)


def _deltanet(proj, pack, packt, conv_w, norm_g, batch, seq):
    nc = seq // CHUNK
    assert nc % PHASE_A_GROUP == 0
    blk = (1, seq, LANES)
    masks = jnp.asarray(_tri_masks(), MXU_DTYPE)
    items = batch * DN_HEADS

    def cur_item(t):
        i = jnp.minimum(t, items - 1)
        return i % DN_HEADS, i // DN_HEADS

    def prev_item(t):
        i = jnp.maximum(t - 1, 0)
        return i % DN_HEADS, i // DN_HEADS

    def col(j0):
        return pl.BlockSpec(blk, lambda t: (j0 + cur_item(t)[0], cur_item(t)[1], 0))

    def cw(j0):
        return pl.BlockSpec((4, LANES), lambda t: (0, j0 + cur_item(t)[0]))

    return pl.pallas_call(
        _dn_kernel,
        name="deltanet",
        grid=(items + 1,),
        in_specs=[col(0), col(DN_HEADS), col(2 * DN_HEADS),
                  pl.BlockSpec(blk, lambda t: (cur_item(t)[0], cur_item(t)[1], 0)),
                  pl.BlockSpec((1, PACKT_ROWS, seq), lambda t: (cur_item(t)[0], 0, cur_item(t)[1])),
                  cw(0), cw(DN_HEADS), cw(2 * DN_HEADS),
                  pl.BlockSpec(blk, lambda t: (3 * DN_HEADS + prev_item(t)[0], prev_item(t)[1], 0)),
                  pl.BlockSpec((1, LANES), lambda t: (0, 0)),
                  pl.BlockSpec(masks.shape, lambda t: (0, 0, 0, 0))],
        out_specs=pl.BlockSpec(blk, lambda t: (prev_item(t)[0], prev_item(t)[1], 0)),
        out_shape=jax.ShapeDtypeStruct((DN_HEADS, batch * seq, LANES), MXU_DTYPE),
        scratch_shapes=[
            pltpu.VMEM((seq + 2 * SUBLANES, LANES), F32),
            pltpu.VMEM((seq, LANES), F32),
            pltpu.VMEM((seq, LANES), F32),
            pltpu.VMEM((seq, LANES), F32),
            pltpu.VMEM((2, 2, nc, DN_HEAD_DIM, DN_HEAD_DIM), F32),
            pltpu.VMEM((2, 2, nc, 2 * CHUNK, LANES), MXU_DTYPE),
            pltpu.VMEM((2, 2, nc, SUBLANES, LANES), F32),
            pltpu.VMEM((2, seq, LANES), F32),
        ],
        compiler_params=pltpu.CompilerParams(dimension_semantics=("arbitrary",),
                                             vmem_limit_bytes=VMEM_LIMIT),
    )(proj, proj, proj, pack, packt, conv_w, conv_w, conv_w, proj, norm_g, masks)


def _scan_level_sizes(seq):
    sizes = [seq]
    while sizes[-1] > SCAN_BOTTOM_ROWS:
        assert sizes[-1] % (SCAN_FANOUT * SUBLANES) == 0
        sizes.append(sizes[-1] // SCAN_FANOUT)
    return sizes


def _lru_kernel(lx_ref, lg_ref, cw_ref, cb_ref, wg_ref, bg_ref, lam_ref, y_ref, pad_ref, h_ref, *lvl):
    nlev = (len(lvl) + 1) // 2
    ab = lvl[:nlev]
    cs = (None,) + tuple(lvl[nlev:])
    seq = lx_ref.shape[1]
    fan = SCAN_FANOUT
    sizes = [r.shape[2] for r in ab]

    xc = _seq_conv(lx_ref[0].astype(F32), cw_ref[...], 2, pad_ref) + cb_ref[...]
    gates = _mm(xc, wg_ref[0]) + bg_ref[0]
    for d in range(2):
        r = _sigmoid(gates[:, (2 * d) * LANES:(2 * d + 1) * LANES])
        ig = _sigmoid(gates[:, (2 * d + 1) * LANES:(2 * d + 2) * LANES])
        log_a = -LRU_C * r * jax.nn.softplus(-lam_ref[0, d:d + 1, :])
        a = jnp.exp(log_a)
        ab[0][d, 0] = a
        v = -jnp.tanh(log_a) * (a * a + 1.0)
        ab[0][d, 1] = jnp.where(v > 0.0, v * lax.rsqrt(v), 0.0) * (ig * xc)
        for l in range(1, nlev):
            cs[l][d, 0:SUBLANES, :] = jnp.zeros((SUBLANES, LANES), F32)
            cs[l][d, SUBLANES + sizes[l]:, :] = jnp.zeros((SUBLANES, LANES), F32)

    def slab(l, j):
        return pl.ds(j, sizes[l] // fan, stride=fan)

    for l in range(nlev - 1):
        for d in range(2):
            order = list(range(fan)) if d == 0 else list(range(fan - 1, -1, -1))
            p = ab[l][d, 0, slab(l, order[0]), :]
            h = ab[l][d, 1, slab(l, order[0]), :]
            for j in order[1:]:
                aj = ab[l][d, 0, slab(l, j), :]
                h = aj * h + ab[l][d, 1, slab(l, j), :]
                p = aj * p
                ab[l][d, 0, slab(l, j), :] = p
                ab[l][d, 1, slab(l, j), :] = h
            ab[l + 1][d, 0] = p
            ab[l + 1][d, 1] = h

    top = nlev - 1
    rows = sizes[top]
    row = _row_iota((rows, LANES))
    for d in range(2):
        a = ab[top][d, 0]
        b = ab[top][d, 1]
        s = 1
        while s < rows:
            sh = rows - s if d == 1 else s
            keep = (row < rows - s) if d == 1 else (row >= s)
            b = jnp.where(keep, a * pltpu.roll(b, sh, 0) + b, b)
            a = jnp.where(keep, a * pltpu.roll(a, sh, 0), a)
            s *= 2
        if top == 0:
            ab[0][d, 1] = b
        else:
            cs[top][d, SUBLANES:SUBLANES + rows, :] = b

    for l in range(nlev - 2, -1, -1):
        n = sizes[l] // fan
        carry = [cs[l + 1][d, SUBLANES - 1 + 2 * d:SUBLANES - 1 + 2 * d + n, :] for d in range(2)]
        for j in range(fan):
            true = [ab[l][d, 1, slab(l, j), :] + ab[l][d, 0, slab(l, j), :] * carry[d] for d in range(2)]
            if l == 0:
                h_ref[slab(0, j), :] = true[0] + true[1]
            else:
                for d in range(2):
                    cs[l][d, pl.ds(SUBLANES + j, n, stride=fan), :] = true[d]
    if nlev == 1:
        h_ref[...] = ab[0][0, 1] + ab[0][1, 1]
    y_ref[0] = (jax.nn.gelu(lg_ref[0].astype(F32)) * h_ref[...]).astype(y_ref.dtype)


def _rglru(proj, conv_w, conv_b, wgate, bgate, lam, batch, seq):
    nblk = LRU_WIDTH // LANES
    blk = (1, seq, LANES)
    j_lx = 4 * DN_HEADS
    j_lg = j_lx + nblk
    sizes = _scan_level_sizes(seq)
    return pl.pallas_call(
        _lru_kernel,
        name="rglru",
        grid=(batch, nblk),
        in_specs=[
            pl.BlockSpec(blk, lambda b, c: (j_lx + c, b, 0)),
            pl.BlockSpec(blk, lambda b, c: (j_lg + c, b, 0)),
            pl.BlockSpec((4, LANES), lambda b, c: (0, c)),
            pl.BlockSpec((1, LANES), lambda b, c: (0, c)),
            pl.BlockSpec((1, LANES, 4 * LANES), lambda b, c: (c, 0, 0)),
            pl.BlockSpec((1, 1, 4 * LANES), lambda b, c: (c, 0, 0)),
            pl.BlockSpec((1, 2, LANES), lambda b, c: (c, 0, 0)),
        ],
        out_specs=pl.BlockSpec(blk, lambda b, c: (c, b, 0)),
        out_shape=jax.ShapeDtypeStruct((nblk, batch * seq, LANES), MXU_DTYPE),
        scratch_shapes=(
            [pltpu.VMEM((seq + 2 * SUBLANES, LANES), F32), pltpu.VMEM((seq, LANES), F32)]
            + [pltpu.VMEM((2, 2, m, LANES), F32) for m in sizes]
            + [pltpu.VMEM((2, m + 2 * SUBLANES, LANES), F32) for m in sizes[1:]]),
        compiler_params=pltpu.CompilerParams(dimension_semantics=("arbitrary", "arbitrary"),
                                             vmem_limit_bytes=VMEM_LIMIT),
    )(proj, proj, conv_w, conv_b, wgate, bgate, lam)


def _mixer_kernel(rp_ref, rm_ref, rn_ref, dnp_ref, dnm_ref, dnn_ref, lyp_ref, lym_ref, lyn_ref, p_ref,
                  lng_ref, wo_ref, g2_ref, wg_ref, wu_ref, cw_ref, cb_ref, wd_ref,
                  png_ref, pwg_ref, pbg_ref, pwp_ref, fg_ref, out_ref, mix_ref, hcat_ref, *, final):
    t = pl.program_id(1)
    tm = rm_ref.shape[1]
    halo = BF16_ROWS
    segments = ((0, halo, rp_ref, dnp_ref, lyp_ref), (halo, tm, rm_ref, dnm_ref, lym_ref),
                (halo + tm, halo, rn_ref, dnn_ref, lyn_ref))
    for lo, n, _, dn_ref, ly_ref in segments:
        for j in range(DN_HEADS):
            mix_ref[lo:lo + n, j * LANES:(j + 1) * LANES] = dn_ref[j]
        ly = jnp.concatenate([ly_ref[j] for j in range(ly_ref.shape[0])], axis=1).astype(F32)
        mix_ref[lo:lo + n, DN_WIDTH:] = _rms(ly, lng_ref[...]).astype(mix_ref.dtype)
    proj = jnp.dot(mix_ref[...], wo_ref[...], preferred_element_type=F32)
    r1 = rm_ref[0] + proj[halo:halo + tm]
    hcat_ref[halo:halo + tm, :] = _rms(r1, g2_ref[...]).astype(hcat_ref.dtype)
    h_prev = _rms(rp_ref[0] + proj[:halo], g2_ref[...])
    h_next = _rms(rn_ref[0] + proj[halo + tm:], g2_ref[...])
    hcat_ref[0:halo, :] = jnp.where(t > 0, h_prev, 0.0).astype(hcat_ref.dtype)
    hcat_ref[halo + tm:, :] = jnp.where(t < pl.num_programs(1) - 1, h_next, 0.0).astype(hcat_ref.dtype)

    rows = tm + 2 * halo
    acc = r1
    for c, width in enumerate(FF_CHUNKS):
        cs = slice(sum(FF_CHUNKS[:c]), sum(FF_CHUNKS[:c]) + width)
        gfull = jnp.dot(hcat_ref[...], wg_ref[:, cs], preferred_element_type=F32)
        gate = (pltpu.roll(gfull, 1, 0)[halo:halo + tm] * cw_ref[0:1, cs]
                + gfull[halo:halo + tm] * cw_ref[1:2, cs]
                + pltpu.roll(gfull, rows - 1, 0)[halo:halo + tm] * cw_ref[2:3, cs]
                + cb_ref[:, cs])
        up = jnp.dot(hcat_ref[halo:halo + tm, :], wu_ref[:, cs], preferred_element_type=F32)
        acc = acc + _mm(jax.nn.gelu(gate) * up, wd_ref[cs, :])
    pgate = _sigmoid(_mm(_rms(acc, png_ref[...]), pwg_ref[...]) + pbg_ref[...])
    res = acc + pgate * _mm(p_ref[0, 0], pwp_ref[...])
    out_ref[0] = _rms(res, fg_ref[...]) if final else res


def _mixer(r, dn, ly, p, layer, final, lng, wo, g2, wg, wu, cw, cb, wd, png, pwg, pbg, pwp, fg):
    batch, seq, _ = r.shape
    tm = TOKEN_TILE
    nt = seq // tm
    hb = tm // BF16_ROWS
    nhb = seq // BF16_ROWS
    nb = LRU_WIDTH // LANES

    def prev_blk(t):
        return jnp.maximum(t * hb - 1, 0)

    def next_blk(t):
        return jnp.minimum((t + 1) * hb, nhb - 1)

    def whole(shape):
        return pl.BlockSpec(shape, lambda b, t: (0,) * len(shape))

    def tok3(nlead):
        return [pl.BlockSpec((nlead, BF16_ROWS, LANES), lambda b, t: (0, b * nhb + prev_blk(t), 0)),
                pl.BlockSpec((nlead, tm, LANES), lambda b, t: (0, b * nt + t, 0)),
                pl.BlockSpec((nlead, BF16_ROWS, LANES), lambda b, t: (0, b * nhb + next_blk(t), 0))]

    return pl.pallas_call(
        functools.partial(_mixer_kernel, final=final),
        name="mixer",
        grid=(batch, nt),
        in_specs=[
            pl.BlockSpec((1, BF16_ROWS, D_MODEL), lambda b, t: (b, prev_blk(t), 0)),
            pl.BlockSpec((1, tm, D_MODEL), lambda b, t: (b, t, 0)),
            pl.BlockSpec((1, BF16_ROWS, D_MODEL), lambda b, t: (b, next_blk(t), 0)),
            *tok3(DN_HEADS), *tok3(nb),
            pl.BlockSpec((1, 1, tm, PLE_DIM), lambda b, t: (layer, b, t, 0)),
            whole((1, LRU_WIDTH)), whole((D_MODEL, D_MODEL)), whole((1, D_MODEL)),
            whole((D_MODEL, D_FF)), whole((D_MODEL, D_FF)), whole((3, D_FF)), whole((1, D_FF)),
            whole((D_FF, D_MODEL)), whole((1, D_MODEL)), whole((D_MODEL, D_MODEL)),
            whole((1, D_MODEL)), whole((PLE_DIM, D_MODEL)), whole((1, D_MODEL)),
        ],
        out_specs=pl.BlockSpec((1, tm, D_MODEL), lambda b, t: (b, t, 0)),
        out_shape=jax.ShapeDtypeStruct((batch, seq, D_MODEL), F32),
        scratch_shapes=[pltpu.VMEM((tm + 2 * BF16_ROWS, D_MODEL), MXU_DTYPE),
                        pltpu.VMEM((tm + 2 * BF16_ROWS, D_MODEL), MXU_DTYPE)],
        compiler_params=pltpu.CompilerParams(dimension_semantics=("arbitrary", "arbitrary"),
                                             vmem_limit_bytes=VMEM_LIMIT),
    )(r, r, r, dn, dn, dn, ly, ly, ly, p, lng, wo, g2, wg, wu, cw, cb, wd, png, pwg, pbg, pwp, fg)


def _row(v):
    return v.reshape(1, -1).astype(F32)


def _gate_lane_row(v):
    return jnp.zeros((1, LANES), F32).at[0, 8:16].set(v.reshape(-1).astype(F32))


def _lru_gate_weights(wa, ba, wx, bx):
    nblk = LRU_WIDTH // LANES
    per = LANES // LRU_BLOCK

    def bd(w):
        w = w.reshape(nblk, per, LRU_BLOCK, LRU_BLOCK)
        eye = jnp.eye(per, dtype=w.dtype)
        return jnp.einsum('npcd,pq->npcqd', w, eye).reshape(nblk, LANES, LANES)

    wcat = jnp.concatenate([bd(wa[0]), bd(wx[0]), bd(wa[1]), bd(wx[1])], axis=2)
    bcat = jnp.concatenate([ba[0].reshape(nblk, 1, LANES), bx[0].reshape(nblk, 1, LANES),
                            ba[1].reshape(nblk, 1, LANES), bx[1].reshape(nblk, 1, LANES)], axis=2)
    return wcat.astype(MXU_DTYPE), bcat.astype(F32)


def kernel(x, p, norm1_g, w_in, dn_conv_w, dn_a_log, dn_dt_bias, dn_norm_g, lru_conv_w, lru_conv_b,
           lru_wa, lru_ba, lru_wx, lru_bx, lru_lambda, lru_norm_g, w_out, norm2_g, ffn_wg, ffn_wu,
           ffn_conv_w, ffn_conv_b, ffn_wd, ple_norm_g, ple_wg, ple_bg, ple_wp, final_g):
    batch, seq, _ = x.shape
    depth = w_in.shape[0]
    assert seq % TOKEN_TILE == 0 and seq % CHUNK == 0 and (batch * seq) % INPROJ_TILE == 0
    nblk = LRU_WIDTH // LANES
    r = x
    for i in range(depth):
        w_perm = jnp.concatenate(
            [w_in[i][:, :BETA_OFF], w_in[i][:, LX_OFF:], w_in[i][:, BETA_OFF:LX_OFF],
             jnp.zeros((D_MODEL, LANES - (LX_OFF - BETA_OFF)), w_in.dtype)], axis=1).astype(MXU_DTYPE)
        proj, pack, packt = _inproj(r.reshape(batch * seq, D_MODEL), _row(norm1_g[i]), w_perm,
                                    _gate_lane_row(dn_a_log[i]), _gate_lane_row(dn_dt_bias[i]))
        dn = _deltanet(proj, pack, packt, dn_conv_w[i], _row(dn_norm_g[i]), batch, seq)
        wgate, bgate = _lru_gate_weights(lru_wa[i], lru_ba[i], lru_wx[i], lru_bx[i])
        lam = jnp.transpose(lru_lambda[i].reshape(2, nblk, LANES), (1, 0, 2)).astype(F32)
        ly = _rglru(proj, lru_conv_w[i], _row(lru_conv_b[i]), wgate, bgate, lam, batch, seq)
        r = _mixer(r, dn, ly, p, i, i == depth - 1, _row(lru_norm_g[i]), w_out[i].astype(MXU_DTYPE),
                   _row(norm2_g[i]), ffn_wg[i].astype(MXU_DTYPE), ffn_wu[i].astype(MXU_DTYPE),
                   ffn_conv_w[i], _row(ffn_conv_b[i]), ffn_wd[i].astype(MXU_DTYPE), _row(ple_norm_g[i]),
                   ple_wg[i].astype(MXU_DTYPE), _row(ple_bg[i]), ple_wp[i].astype(MXU_DTYPE), _row(final_g))
    return r
```

```python
import functools

import jax
import numpy as np
import jax.numpy as jnp
from jax import lax
from jax.experimental import pallas as pl
from jax.experimental.pallas import tpu as pltpu

D_MODEL = 1024
DN_HEADS = 4
DN_HEAD_DIM = 128
DN_WIDTH = DN_HEADS * DN_HEAD_DIM
LRU_WIDTH = 512
LRU_BLOCKS = 8
LRU_BLOCK = LRU_WIDTH // LRU_BLOCKS
LRU_C = 8.0
D_FF = 2816
PLE_DIM = 256
EPS = 1e-6
Z_OFF = 3 * DN_WIDTH
BETA_OFF = 4 * DN_WIDTH
LX_OFF = BETA_OFF + 4 * DN_HEADS
IN_COLS = LX_OFF + 2 * LRU_WIDTH

LANES = 128
SUBLANES = 8
BF16_ROWS = 16
CHUNK = 128
PACKT_ROWS = 8
SCAN_FANOUT = 4
SCAN_BOTTOM_ROWS = 32
PHASE_A_GROUP = 8
MAIN_COLS = 4 * DN_WIDTH + 2 * LRU_WIDTH
N_MAIN_BLK = MAIN_COLS // LANES
PROJ_COLS = MAIN_COLS + LANES
TOKEN_TILE = 512
INPROJ_TILE = 2 * TOKEN_TILE
MXU_TILE = 256
FF_CHUNKS = (6 * MXU_TILE, 5 * MXU_TILE)
assert sum(FF_CHUNKS) == D_FF
VMEM_LIMIT = 56 * 1024 * 1024

MXU_DTYPE = jnp.bfloat16
F32 = jnp.float32


def _mm(a, b):
    return jnp.dot(a.astype(MXU_DTYPE), b.astype(MXU_DTYPE), preferred_element_type=F32)


def _mm_nt(a, b):
    return lax.dot_general(a.astype(MXU_DTYPE), b.astype(MXU_DTYPE), (((1,), (1,)), ((), ())),
                           preferred_element_type=F32)


def _rms(x, g):
    return x * lax.rsqrt(jnp.mean(x * x, axis=-1, keepdims=True) + EPS) * g


def _row_iota(shape):
    return lax.broadcasted_iota(jnp.int32, shape, 0)


def _col_iota(shape):
    return lax.broadcasted_iota(jnp.int32, shape, 1)


def _sigmoid(x):
    return 0.5 * jnp.tanh(0.5 * x) + 0.5


def _inproj_kernel(rb_ref, r_ref, ra_ref, g_ref, w_ref, alog_ref, dtb_ref, dcw_ref, lcw_ref, lcb_ref,
                   proj_ref, pack_ref, packt_ref, stage_ref, *, tiles_per_seq):
    tile_pos = pl.program_id(0) % tiles_per_seq
    tm = r_ref.shape[0]
    halo = SUBLANES
    sq = (CHUNK, CHUNK)
    tri = jnp.concatenate([(_col_iota(sq) <= _row_iota(sq)).astype(MXU_DTYPE),
                           (_col_iota(sq) >= _row_iota(sq)).astype(MXU_DTYPE)], axis=0)
    lane = _col_iota((CHUNK, LANES))

    def conv(res, j, w4):
        stage_ref[j] = res[:, j * LANES:(j + 1) * LANES]
        out = None
        for tap in range(w4.shape[0]):
            lo = halo - 2 + tap
            term = stage_ref[j, lo:lo + TOKEN_TILE, :] * w4[tap:tap + 1, :]
            out = term if out is None else out + term
        return out

    def half_silu(hy):
        return hy + hy * jnp.tanh(hy)

    def l2n(x):
        return x * lax.rsqrt(jnp.sum(x * x, axis=-1, keepdims=True) + EPS)

    for s0 in range(0, tm, TOKEN_TILE):
        if s0 == 0:
            before = jnp.where(tile_pos > 0, _rms(rb_ref[...], g_ref[...]), 0.0)
        else:
            before = _rms(r_ref[s0 - halo:s0, :], g_ref[...])
        if s0 + TOKEN_TILE == tm:
            after = jnp.where(tile_pos < tiles_per_seq - 1, _rms(ra_ref[...], g_ref[...]), 0.0)
        else:
            after = _rms(r_ref[s0 + TOKEN_TILE:s0 + TOKEN_TILE + halo, :], g_ref[...])
        h = jnp.concatenate([before, _rms(r_ref[s0:s0 + TOKEN_TILE, :], g_ref[...]), after], axis=0)
        res = _mm(h, w_ref[...])
        rows = slice(s0, s0 + TOKEN_TILE)
        for j in range(N_MAIN_BLK):
            cols = slice(j * LANES, (j + 1) * LANES)
            if j < 3 * DN_HEADS:
                y = half_silu(conv(res, j, 0.5 * dcw_ref[:, cols]))
                if j < DN_HEADS:
                    y = l2n(y) * (DN_HEAD_DIM ** -0.5)
                elif j < 2 * DN_HEADS:
                    y = l2n(y)
            elif 4 * DN_HEADS <= j < 4 * DN_HEADS + LRU_WIDTH // LANES:
                lc = slice((j - 4 * DN_HEADS) * LANES, (j - 4 * DN_HEADS + 1) * LANES)
                y = conv(res, j, lcw_ref[:, lc]) + lcb_ref[:, lc]
            else:
                y = res[halo:halo + TOKEN_TILE, cols]
            proj_ref[j, rows, :] = y.astype(proj_ref.dtype)
        gates = res[halo:halo + TOKEN_TILE, MAIN_COLS:]
        beta = _sigmoid(gates)
        g = -jnp.exp(alog_ref[...]) * jax.nn.softplus(gates + dtb_ref[...])
        for c in range(TOKEN_TILE // CHUNK):
            crow = slice(s0 + c * CHUNK, s0 + (c + 1) * CHUNK)
            gc = g[c * CHUNK:(c + 1) * CHUNK]
            p1 = gc.astype(MXU_DTYPE)
            r1 = gc - p1.astype(F32)
            p2 = r1.astype(MXU_DTYPE)
            p3 = (r1 - p2.astype(F32)).astype(MXU_DTYPE)
            cs = jnp.dot(tri, jnp.concatenate([p1, p2, p3], axis=1), preferred_element_type=F32)
            cs = cs[:, :LANES] + cs[:, LANES:2 * LANES] + cs[:, 2 * LANES:]
            prefix, suffix = cs[:CHUNK], cs[CHUNK:]
            pk = jnp.where(lane < 8, beta[c * CHUNK:(c + 1) * CHUNK],
                           jnp.where(lane < 12, prefix, suffix))
            for hd in range(DN_HEADS):
                pack_ref[hd, crow, :] = pk if hd == 0 else pltpu.roll(pk, LANES - hd, 1)
            pkt = pk.T
            for hd in range(DN_HEADS):
                packt_ref[hd, :, crow] = jnp.concatenate(
                    [pkt[8 + hd:9 + hd], pkt[12 + hd:13 + hd], jnp.zeros((PACKT_ROWS - 2, CHUNK), F32)], axis=0)


def _inproj(r2d, seq, g1, w, alog, dtb, dn_conv_w, lru_conv_w, lru_conv_b):
    t = r2d.shape[0]
    tm = INPROJ_TILE
    hb = tm // SUBLANES
    nhb = t // SUBLANES

    def whole(shape):
        return pl.BlockSpec(shape, lambda i: (0,) * len(shape))

    return pl.pallas_call(
        functools.partial(_inproj_kernel, tiles_per_seq=seq // tm),
        name="inproj",
        grid=(t // tm,),
        in_specs=[
            pl.BlockSpec((SUBLANES, D_MODEL), lambda i: (jnp.maximum(i * hb - 1, 0), 0)),
            pl.BlockSpec((tm, D_MODEL), lambda i: (i, 0)),
            pl.BlockSpec((SUBLANES, D_MODEL), lambda i: (jnp.minimum((i + 1) * hb, nhb - 1), 0)),
            whole((1, D_MODEL)), whole((D_MODEL, PROJ_COLS)), whole((1, LANES)), whole((1, LANES)),
            whole(dn_conv_w.shape), whole(lru_conv_w.shape), whole(lru_conv_b.shape),
        ],
        out_specs=[
            pl.BlockSpec((N_MAIN_BLK, tm, LANES), lambda i: (0, i, 0)),
            pl.BlockSpec((DN_HEADS, tm, LANES), lambda i: (0, i, 0)),
            pl.BlockSpec((DN_HEADS, PACKT_ROWS, tm), lambda i: (0, 0, i)),
        ],
        out_shape=[
            jax.ShapeDtypeStruct((N_MAIN_BLK, t, LANES), MXU_DTYPE),
            jax.ShapeDtypeStruct((DN_HEADS, t, LANES), F32),
            jax.ShapeDtypeStruct((DN_HEADS, PACKT_ROWS, t), F32),
        ],
        scratch_shapes=[pltpu.VMEM((N_MAIN_BLK, TOKEN_TILE + 2 * SUBLANES, LANES), F32)],
        compiler_params=pltpu.CompilerParams(dimension_semantics=("arbitrary",),
                                             vmem_limit_bytes=VMEM_LIMIT),
    )(r2d, r2d, r2d, g1, w, alog, dtb, dn_conv_w, lru_conv_w, lru_conv_b)


N_TRI_MASKS = 6


def _tri_masks():
    i = np.arange(CHUNK)[:, None]
    j = np.arange(CHUNK)[None, :]
    out = np.zeros((2, N_TRI_MASKS, CHUNK, CHUNK), np.float32)
    for d in range(2):
        hi, lo = (j, i) if d == 1 else (i, j)
        out[d, 0] = (hi // SUBLANES == lo // SUBLANES) & (hi > lo)
        s, lvl = SUBLANES, 1
        while s < CHUNK:
            out[d, lvl] = (hi // (2 * s) == lo // (2 * s)) & ((hi // s) % 2 == 1) & ((lo // s) % 2 == 0)
            s, lvl = 2 * s, lvl + 1
        out[d, N_TRI_MASKS - 1] = hi >= lo
    return out


def _unit_tri_inverse_minus_eye(a_list, mask_of, tick):
    idx = range(len(a_list))
    x = [a_list[i] * mask_of(i, 0) for i in idx]
    y2 = [_mm(x[i], x[i]) for i in idx]
    tick()
    y4 = [_mm(y2[i], y2[i]) for i in idx]
    tick()
    xy2 = [_mm(x[i], y2[i]) for i in idx]
    tick()
    m1 = [y2[i] - x[i].astype(F32) - xy2[i] for i in idx]
    m1y4 = [_mm(m1[i], y4[i]) for i in idx]
    tick()
    n = [m1[i] + y4[i] + m1y4[i] for i in idx]
    for lvl in range(1, N_TRI_MASKS - 1):
        b = [a_list[i] * mask_of(i, lvl) for i in idx]
        p = [b[i].astype(F32) + _mm(n[i], b[i]) for i in idx]
        tick()
        pn = [_mm(p[i], n[i]) for i in idx]
        tick()
        n = [n[i] - p[i] - pn[i] for i in idx]
    return n


def _dn_kernel(q_ref, k_ref, v_ref, pack_ref, packt_ref, z_ref, ng_ref, mask_ref, out_ref,
               c_ref, nmq_ref, cd_ref, o_ref):
    t = pl.program_id(0)
    cur = t % 2
    prv = 1 - cur
    seq = q_ref.shape[1]
    nc = seq // CHUNK

    @pl.when(t == 0)
    def _():
        c_ref[1] = jnp.zeros(c_ref.shape[1:], c_ref.dtype)
        nmq_ref[1] = jnp.zeros(nmq_ref.shape[1:], nmq_ref.dtype)
        cd_ref[1] = jnp.zeros(cd_ref.shape[1:], cd_ref.dtype)
        o_ref[1] = jnp.zeros(o_ref.shape[1:], o_ref.dtype)

    o_ref[cur] = jnp.zeros(o_ref.shape[1:], o_ref.dtype)

    sq = (CHUNK, CHUNK)
    eye = (_row_iota(sq) == _col_iota(sq)).astype(MXU_DTYPE)

    def fused(gi, states):
        states = list(states)
        b_done = [0]

        def tick():
            if b_done[0] == PHASE_A_GROUP:
                return
            i = gi * PHASE_A_GROUP + b_done[0]
            b_done[0] += 1
            ns = (i, nc - 1 - i)
            rs = [_mm(nmq_ref[prv, d, ns[d]], states[d]) for d in range(2)]
            for d in range(2):
                o_ref[prv, pl.ds(pl.multiple_of(ns[d] * CHUNK, CHUNK), CHUNK), :] += rs[d][CHUNK:]
                states[d] = states[d] * cd_ref[prv, d, ns[d]][0:1, :] + rs[d][:CHUNK] + c_ref[prv, d, ns[d]]

        chunks = []
        for c in range(PHASE_A_GROUP):
            n = gi * PHASE_A_GROUP + c
            off = pl.multiple_of(n * CHUNK, CHUNK)
            chunks.append((n, off, q_ref[0, pl.ds(off, CHUNK), :], k_ref[0, pl.ds(off, CHUNK), :]))
        kqs = [_mm_nt(jnp.concatenate([kc, qc, eye], axis=0), kc) for (_, _, qc, kc) in chunks]
        tick()
        inst = []
        for (n, off, qc, kc), kq in zip(chunks, kqs):
            pk = pack_ref[0, pl.ds(off, CHUNK), :]
            vc = v_ref[0, pl.ds(off, CHUNK), :].astype(F32)
            qc = qc.astype(F32)
            kc = kc.astype(F32)
            for d in range(2):
                beta = jnp.broadcast_to(pk[:, 4 * d:4 * d + 1], sq)
                gc = jnp.broadcast_to(pk[:, 8 + 4 * d:9 + 4 * d], sq)
                gc_row = packt_ref[0, d:d + 1, pl.ds(off, CHUNK)]
                glast = gc[0:1, :] if d == 1 else gc[CHUNK - 1:CHUNK, :]
                e_in = jnp.exp(gc)
                decay = jnp.exp(jnp.minimum(gc - gc_row, 0.0)) * mask_ref[d, N_TRI_MASKS - 1].astype(F32)
                attn = (kq[CHUNK:2 * CHUNK] * decay).astype(MXU_DTYPE)
                kdt = (kq[2 * CHUNK:] * jnp.exp(glast - gc_row)).astype(MXU_DTYPE)
                cd_ref[cur, d, n] = jnp.exp(jnp.broadcast_to(glast, (SUBLANES, LANES)))
                rhs = jnp.concatenate([vc * beta, kc * (beta * e_in)], axis=1).astype(MXU_DTYPE)
                a_raw = (beta * kq[:CHUNK] * decay).astype(MXU_DTYPE)
                inst.append((d, n, off, a_raw, rhs, attn, kdt, qc * e_in))
        idx = range(len(inst))
        n_inv = _unit_tri_inverse_minus_eye([v[3] for v in inst], lambda i, m: mask_ref[inst[i][0], m], tick)
        assert b_done[0] == PHASE_A_GROUP
        uw = [(inst[i][4].astype(F32) + _mm(n_inv[i], inst[i][4])).astype(MXU_DTYPE) for i in idx]
        kuw = [_mm(inst[i][6], uw[i]) for i in idx]
        auw = [_mm(inst[i][5], uw[i]) for i in idx]
        for i in idx:
            d, n, off = inst[i][:3]
            c_ref[cur, d, n] = kuw[i][:, :DN_HEAD_DIM]
            nmq_ref[cur, d, n, :CHUNK, :] = (-kuw[i][:, DN_HEAD_DIM:]).astype(nmq_ref.dtype)
            nmq_ref[cur, d, n, CHUNK:, :] = (inst[i][7] - auw[i][:, DN_HEAD_DIM:]).astype(nmq_ref.dtype)
            o_ref[cur, pl.ds(off, CHUNK), :] += auw[i][:, :DN_HEAD_DIM]
        return tuple(states)

    zero = jnp.zeros((DN_HEAD_DIM, DN_HEAD_DIM), F32)
    lax.fori_loop(0, nc // PHASE_A_GROUP, fused, (zero, zero))

    z = z_ref[0].astype(F32)
    out_ref[0] = (_rms(o_ref[prv], ng_ref[...]) * (z * _sigmoid(z))).astype(out_ref.dtype)


def _deltanet(proj, pack, packt, norm_g, batch, seq):
    nc = seq // CHUNK
    assert nc % PHASE_A_GROUP == 0
    blk = (1, seq, LANES)
    masks = jnp.asarray(_tri_masks(), MXU_DTYPE)
    items = batch * DN_HEADS

    def cur_item(t):
        i = jnp.minimum(t, items - 1)
        return i % DN_HEADS, i // DN_HEADS

    def prev_item(t):
        i = jnp.maximum(t - 1, 0)
        return i % DN_HEADS, i // DN_HEADS

    def col(j0):
        return pl.BlockSpec(blk, lambda t: (j0 + cur_item(t)[0], cur_item(t)[1], 0))

    return pl.pallas_call(
        _dn_kernel,
        name="deltanet",
        grid=(items + 1,),
        in_specs=[col(0), col(DN_HEADS), col(2 * DN_HEADS),
                  pl.BlockSpec(blk, lambda t: (cur_item(t)[0], cur_item(t)[1], 0)),
                  pl.BlockSpec((1, PACKT_ROWS, seq), lambda t: (cur_item(t)[0], 0, cur_item(t)[1])),
                  pl.BlockSpec(blk, lambda t: (3 * DN_HEADS + prev_item(t)[0], prev_item(t)[1], 0)),
                  pl.BlockSpec((1, LANES), lambda t: (0, 0)),
                  pl.BlockSpec(masks.shape, lambda t: (0, 0, 0, 0))],
        out_specs=pl.BlockSpec(blk, lambda t: (prev_item(t)[0], prev_item(t)[1], 0)),
        out_shape=jax.ShapeDtypeStruct((DN_HEADS, batch * seq, LANES), MXU_DTYPE),
        scratch_shapes=[
            pltpu.VMEM((2, 2, nc, DN_HEAD_DIM, DN_HEAD_DIM), F32),
            pltpu.VMEM((2, 2, nc, 2 * CHUNK, LANES), MXU_DTYPE),
            pltpu.VMEM((2, 2, nc, SUBLANES, LANES), F32),
            pltpu.VMEM((2, seq, LANES), F32),
        ],
        compiler_params=pltpu.CompilerParams(dimension_semantics=("arbitrary",),
                                             vmem_limit_bytes=VMEM_LIMIT),
    )(proj, proj, proj, pack, packt, proj, norm_g, masks)


def _scan_level_sizes(seq):
    sizes = [seq]
    while sizes[-1] > SCAN_BOTTOM_ROWS:
        assert sizes[-1] % (SCAN_FANOUT * SUBLANES) == 0
        sizes.append(sizes[-1] // SCAN_FANOUT)
    return sizes


def _lru_kernel(lx_ref, lg_ref, wg_ref, bg_ref, lam_ref, y_ref, h_ref, *lvl):
    nlev = (len(lvl) + 1) // 2
    ab = lvl[:nlev]
    cs = (None,) + tuple(lvl[nlev:])
    seq = lx_ref.shape[1]
    fan = SCAN_FANOUT
    sizes = [r.shape[2] for r in ab]

    xc = lx_ref[0].astype(F32)
    gates = _mm(xc, wg_ref[0]) + bg_ref[0]
    for d in range(2):
        r = _sigmoid(gates[:, (2 * d) * LANES:(2 * d + 1) * LANES])
        ig = _sigmoid(gates[:, (2 * d + 1) * LANES:(2 * d + 2) * LANES])
        log_a = -LRU_C * r * jax.nn.softplus(-lam_ref[0, d:d + 1, :])
        a = jnp.exp(log_a)
        ab[0][d, 0] = a
        v = -jnp.tanh(log_a) * (a * a + 1.0)
        ab[0][d, 1] = jnp.where(v > 0.0, v * lax.rsqrt(v), 0.0) * (ig * xc)
        for l in range(1, nlev):
            cs[l][d, 0:SUBLANES, :] = jnp.zeros((SUBLANES, LANES), F32)
            cs[l][d, SUBLANES + sizes[l]:, :] = jnp.zeros((SUBLANES, LANES), F32)

    def slab(l, j):
        return pl.ds(j, sizes[l] // fan, stride=fan)

    for l in range(nlev - 1):
        for d in range(2):
            order = list(range(fan)) if d == 0 else list(range(fan - 1, -1, -1))
            p = ab[l][d, 0, slab(l, order[0]), :]
            h = ab[l][d, 1, slab(l, order[0]), :]
            for j in order[1:]:
                aj = ab[l][d, 0, slab(l, j), :]
                h = aj * h + ab[l][d, 1, slab(l, j), :]
                p = aj * p
                ab[l][d, 0, slab(l, j), :] = p
                ab[l][d, 1, slab(l, j), :] = h
            ab[l + 1][d, 0] = p
            ab[l + 1][d, 1] = h

    top = nlev - 1
    rows = sizes[top]
    row = _row_iota((rows, LANES))
    for d in range(2):
        a = ab[top][d, 0]
        b = ab[top][d, 1]
        s = 1
        while s < rows:
            sh = rows - s if d == 1 else s
            keep = (row < rows - s) if d == 1 else (row >= s)
            b = jnp.where(keep, a * pltpu.roll(b, sh, 0) + b, b)
            a = jnp.where(keep, a * pltpu.roll(a, sh, 0), a)
            s *= 2
        if top == 0:
            ab[0][d, 1] = b
        else:
            cs[top][d, SUBLANES:SUBLANES + rows, :] = b

    for l in range(nlev - 2, -1, -1):
        n = sizes[l] // fan
        carry = [cs[l + 1][d, SUBLANES - 1 + 2 * d:SUBLANES - 1 + 2 * d + n, :] for d in range(2)]
        for j in range(fan):
            true = [ab[l][d, 1, slab(l, j), :] + ab[l][d, 0, slab(l, j), :] * carry[d] for d in range(2)]
            if l == 0:
                h_ref[slab(0, j), :] = true[0] + true[1]
            else:
                for d in range(2):
                    cs[l][d, pl.ds(SUBLANES + j, n, stride=fan), :] = true[d]
    if nlev == 1:
        h_ref[...] = ab[0][0, 1] + ab[0][1, 1]
    y_ref[0] = (jax.nn.gelu(lg_ref[0].astype(F32)) * h_ref[...]).astype(y_ref.dtype)


def _rglru(proj, wgate, bgate, lam, batch, seq):
    nblk = LRU_WIDTH // LANES
    blk = (1, seq, LANES)
    j_lx = 4 * DN_HEADS
    j_lg = j_lx + nblk
    sizes = _scan_level_sizes(seq)
    return pl.pallas_call(
        _lru_kernel,
        name="rglru",
        grid=(batch, nblk),
        in_specs=[
            pl.BlockSpec(blk, lambda b, c: (j_lx + c, b, 0)),
            pl.BlockSpec(blk, lambda b, c: (j_lg + c, b, 0)),
            pl.BlockSpec((1, LANES, 4 * LANES), lambda b, c: (c, 0, 0)),
            pl.BlockSpec((1, 1, 4 * LANES), lambda b, c: (c, 0, 0)),
            pl.BlockSpec((1, 2, LANES), lambda b, c: (c, 0, 0)),
        ],
        out_specs=pl.BlockSpec(blk, lambda b, c: (c, b, 0)),
        out_shape=jax.ShapeDtypeStruct((nblk, batch * seq, LANES), MXU_DTYPE),
        scratch_shapes=(
            [pltpu.VMEM((seq, LANES), F32)]
            + [pltpu.VMEM((2, 2, m, LANES), F32) for m in sizes]
            + [pltpu.VMEM((2, m + 2 * SUBLANES, LANES), F32) for m in sizes[1:]]),
        compiler_params=pltpu.CompilerParams(dimension_semantics=("arbitrary", "arbitrary"),
                                             vmem_limit_bytes=VMEM_LIMIT),
    )(proj, proj, wgate, bgate, lam)


def _mixer_kernel(rp_ref, rm_ref, rn_ref, dnp_ref, dnm_ref, dnn_ref, lyp_ref, lym_ref, lyn_ref, p_ref,
                  lng_ref, wo_ref, g2_ref, wg_ref, wu_ref, cw_ref, cb_ref, wd_ref,
                  png_ref, pwg_ref, pbg_ref, pwp_ref, fg_ref, out_ref, mix_ref, hcat_ref, *, final):
    t = pl.program_id(1)
    tm = rm_ref.shape[1]
    halo = BF16_ROWS
    segments = ((0, halo, rp_ref, dnp_ref, lyp_ref), (halo, tm, rm_ref, dnm_ref, lym_ref),
                (halo + tm, halo, rn_ref, dnn_ref, lyn_ref))
    for lo, n, _, dn_ref, ly_ref in segments:
        for j in range(DN_HEADS):
            mix_ref[lo:lo + n, j * LANES:(j + 1) * LANES] = dn_ref[j]
        ly = jnp.concatenate([ly_ref[j] for j in range(ly_ref.shape[0])], axis=1).astype(F32)
        mix_ref[lo:lo + n, DN_WIDTH:] = _rms(ly, lng_ref[...]).astype(mix_ref.dtype)
    proj = jnp.dot(mix_ref[...], wo_ref[...], preferred_element_type=F32)
    r1 = rm_ref[0] + proj[halo:halo + tm]
    hcat_ref[halo:halo + tm, :] = _rms(r1, g2_ref[...]).astype(hcat_ref.dtype)
    h_prev = _rms(rp_ref[0] + proj[:halo], g2_ref[...])
    h_next = _rms(rn_ref[0] + proj[halo + tm:], g2_ref[...])
    hcat_ref[0:halo, :] = jnp.where(t > 0, h_prev, 0.0).astype(hcat_ref.dtype)
    hcat_ref[halo + tm:, :] = jnp.where(t < pl.num_programs(1) - 1, h_next, 0.0).astype(hcat_ref.dtype)

    rows = tm + 2 * halo
    acc = r1
    for c, width in enumerate(FF_CHUNKS):
        cs = slice(sum(FF_CHUNKS[:c]), sum(FF_CHUNKS[:c]) + width)
        gfull = jnp.dot(hcat_ref[...], wg_ref[:, cs], preferred_element_type=F32)
        gate = (pltpu.roll(gfull, 1, 0)[halo:halo + tm] * cw_ref[0:1, cs]
                + gfull[halo:halo + tm] * cw_ref[1:2, cs]
                + pltpu.roll(gfull, rows - 1, 0)[halo:halo + tm] * cw_ref[2:3, cs]
                + cb_ref[:, cs])
        up = jnp.dot(hcat_ref[halo:halo + tm, :], wu_ref[:, cs], preferred_element_type=F32)
        acc = acc + _mm(jax.nn.gelu(gate) * up, wd_ref[cs, :])
    pgate = _sigmoid(_mm(_rms(acc, png_ref[...]), pwg_ref[...]) + pbg_ref[...])
    res = acc + pgate * _mm(p_ref[0, 0], pwp_ref[...])
    out_ref[0] = _rms(res, fg_ref[...]) if final else res


def _mixer(r, dn, ly, p, layer, final, lng, wo, g2, wg, wu, cw, cb, wd, png, pwg, pbg, pwp, fg):
    batch, seq, _ = r.shape
    tm = TOKEN_TILE
    nt = seq // tm
    hb = tm // BF16_ROWS
    nhb = seq // BF16_ROWS
    nb = LRU_WIDTH // LANES

    def prev_blk(t):
        return jnp.maximum(t * hb - 1, 0)

    def next_blk(t):
        return jnp.minimum((t + 1) * hb, nhb - 1)

    def whole(shape):
        return pl.BlockSpec(shape, lambda b, t: (0,) * len(shape))

    def tok3(nlead):
        return [pl.BlockSpec((nlead, BF16_ROWS, LANES), lambda b, t: (0, b * nhb + prev_blk(t), 0)),
                pl.BlockSpec((nlead, tm, LANES), lambda b, t: (0, b * nt + t, 0)),
                pl.BlockSpec((nlead, BF16_ROWS, LANES), lambda b, t: (0, b * nhb + next_blk(t), 0))]

    return pl.pallas_call(
        functools.partial(_mixer_kernel, final=final),
        name="mixer",
        grid=(batch, nt),
        in_specs=[
            pl.BlockSpec((1, BF16_ROWS, D_MODEL), lambda b, t: (b, prev_blk(t), 0)),
            pl.BlockSpec((1, tm, D_MODEL), lambda b, t: (b, t, 0)),
            pl.BlockSpec((1, BF16_ROWS, D_MODEL), lambda b, t: (b, next_blk(t), 0)),
            *tok3(DN_HEADS), *tok3(nb),
            pl.BlockSpec((1, 1, tm, PLE_DIM), lambda b, t: (layer, b, t, 0)),
            whole((1, LRU_WIDTH)), whole((D_MODEL, D_MODEL)), whole((1, D_MODEL)),
            whole((D_MODEL, D_FF)), whole((D_MODEL, D_FF)), whole((3, D_FF)), whole((1, D_FF)),
            whole((D_FF, D_MODEL)), whole((1, D_MODEL)), whole((D_MODEL, D_MODEL)),
            whole((1, D_MODEL)), whole((PLE_DIM, D_MODEL)), whole((1, D_MODEL)),
        ],
        out_specs=pl.BlockSpec((1, tm, D_MODEL), lambda b, t: (b, t, 0)),
        out_shape=jax.ShapeDtypeStruct((batch, seq, D_MODEL), F32),
        scratch_shapes=[pltpu.VMEM((tm + 2 * BF16_ROWS, D_MODEL), MXU_DTYPE),
                        pltpu.VMEM((tm + 2 * BF16_ROWS, D_MODEL), MXU_DTYPE)],
        compiler_params=pltpu.CompilerParams(dimension_semantics=("arbitrary", "arbitrary"),
                                             vmem_limit_bytes=VMEM_LIMIT),
    )(r, r, r, dn, dn, dn, ly, ly, ly, p, lng, wo, g2, wg, wu, cw, cb, wd, png, pwg, pbg, pwp, fg)


def _row(v):
    return v.reshape(1, -1).astype(F32)


def _gate_lane_row(v):
    return jnp.zeros((1, LANES), F32).at[0, 8:16].set(v.reshape(-1).astype(F32))


def _lru_gate_weights(wa, ba, wx, bx):
    nblk = LRU_WIDTH // LANES
    per = LANES // LRU_BLOCK

    def bd(w):
        w = w.reshape(nblk, per, LRU_BLOCK, LRU_BLOCK)
        eye = jnp.eye(per, dtype=w.dtype)
        return jnp.einsum('npcd,pq->npcqd', w, eye).reshape(nblk, LANES, LANES)

    wcat = jnp.concatenate([bd(wa[0]), bd(wx[0]), bd(wa[1]), bd(wx[1])], axis=2)
    bcat = jnp.concatenate([ba[0].reshape(nblk, 1, LANES), bx[0].reshape(nblk, 1, LANES),
                            ba[1].reshape(nblk, 1, LANES), bx[1].reshape(nblk, 1, LANES)], axis=2)
    return wcat.astype(MXU_DTYPE), bcat.astype(F32)


def kernel(x, p, norm1_g, w_in, dn_conv_w, dn_a_log, dn_dt_bias, dn_norm_g, lru_conv_w, lru_conv_b,
           lru_wa, lru_ba, lru_wx, lru_bx, lru_lambda, lru_norm_g, w_out, norm2_g, ffn_wg, ffn_wu,
           ffn_conv_w, ffn_conv_b, ffn_wd, ple_norm_g, ple_wg, ple_bg, ple_wp, final_g):
    batch, seq, _ = x.shape
    depth = w_in.shape[0]
    assert seq % INPROJ_TILE == 0 and seq % CHUNK == 0
    nblk = LRU_WIDTH // LANES
    r = x
    for i in range(depth):
        w_perm = jnp.concatenate(
            [w_in[i][:, :BETA_OFF], w_in[i][:, LX_OFF:], w_in[i][:, BETA_OFF:LX_OFF],
             jnp.zeros((D_MODEL, LANES - (LX_OFF - BETA_OFF)), w_in.dtype)], axis=1).astype(MXU_DTYPE)
        proj, pack, packt = _inproj(r.reshape(batch * seq, D_MODEL), seq, _row(norm1_g[i]), w_perm,
                                    _gate_lane_row(dn_a_log[i]), _gate_lane_row(dn_dt_bias[i]),
                                    dn_conv_w[i], lru_conv_w[i], _row(lru_conv_b[i]))
        dn = _deltanet(proj, pack, packt, _row(dn_norm_g[i]), batch, seq)
        wgate, bgate = _lru_gate_weights(lru_wa[i], lru_ba[i], lru_wx[i], lru_bx[i])
        lam = jnp.transpose(lru_lambda[i].reshape(2, nblk, LANES), (1, 0, 2)).astype(F32)
        ly = _rglru(proj, wgate, bgate, lam, batch, seq)
        r = _mixer(r, dn, ly, p, i, i == depth - 1, _row(lru_norm_g[i]), w_out[i].astype(MXU_DTYPE),
                   _row(norm2_g[i]), ffn_wg[i].astype(MXU_DTYPE), ffn_wu[i].astype(MXU_DTYPE),
                   ffn_conv_w[i], _row(ffn_conv_b[i]), ffn_wd[i].astype(MXU_DTYPE), _row(ple_norm_g[i]),
                   ple_wg[i].astype(MXU_DTYPE), _row(ple_bg[i]), ple_wp[i].astype(MXU_DTYPE), _row(final_g))
    return r
```

```python
import functools

import jax
import numpy as np
import jax.numpy as jnp
from jax import lax
from jax.experimental import pallas as pl
from jax.experimental.pallas import tpu as pltpu

D_MODEL = 1024
DN_HEADS = 4
DN_HEAD_DIM = 128
DN_WIDTH = DN_HEADS * DN_HEAD_DIM
LRU_WIDTH = 512
LRU_BLOCKS = 8
LRU_BLOCK = LRU_WIDTH // LRU_BLOCKS
LRU_C = 8.0
D_FF = 2816
PLE_DIM = 256
EPS = 1e-6
Z_OFF = 3 * DN_WIDTH
BETA_OFF = 4 * DN_WIDTH
LX_OFF = BETA_OFF + 4 * DN_HEADS
IN_COLS = LX_OFF + 2 * LRU_WIDTH

LANES = 128
SUBLANES = 8
BF16_ROWS = 16
CHUNK = 128
PACKT_ROWS = 8
SCAN_FANOUT = 4
SCAN_BOTTOM_ROWS = 32
LRU_PIECE_ROWS = 128
LRU_PIECE_EVERY = 9
PHASE_A_GROUP = 8
MAIN_COLS = 4 * DN_WIDTH + 2 * LRU_WIDTH
N_MAIN_BLK = MAIN_COLS // LANES
PROJ_COLS = MAIN_COLS + LANES
TOKEN_TILE = 512
INPROJ_TILE = 2 * TOKEN_TILE
MXU_TILE = 256
FF_CHUNKS = (6 * MXU_TILE, 5 * MXU_TILE)
assert sum(FF_CHUNKS) == D_FF
VMEM_LIMIT = 56 * 1024 * 1024

MXU_DTYPE = jnp.bfloat16
F32 = jnp.float32


def _mm(a, b):
    return jnp.dot(a.astype(MXU_DTYPE), b.astype(MXU_DTYPE), preferred_element_type=F32)


def _mm_nt(a, b):
    return lax.dot_general(a.astype(MXU_DTYPE), b.astype(MXU_DTYPE), (((1,), (1,)), ((), ())),
                           preferred_element_type=F32)


def _rms(x, g):
    return x * lax.rsqrt(jnp.mean(x * x, axis=-1, keepdims=True) + EPS) * g


def _row_iota(shape):
    return lax.broadcasted_iota(jnp.int32, shape, 0)


def _col_iota(shape):
    return lax.broadcasted_iota(jnp.int32, shape, 1)


def _sigmoid(x):
    return 0.5 * jnp.tanh(0.5 * x) + 0.5


def _inproj_kernel(rb_ref, r_ref, ra_ref, g_ref, w_ref, alog_ref, dtb_ref, dcw_ref, lcw_ref, lcb_ref,
                   proj_ref, pack_ref, packt_ref, stage_ref, *, tiles_per_seq):
    tile_pos = pl.program_id(0) % tiles_per_seq
    tm = r_ref.shape[0]
    halo = SUBLANES
    sq = (CHUNK, CHUNK)
    tri = jnp.concatenate([(_col_iota(sq) <= _row_iota(sq)).astype(MXU_DTYPE),
                           (_col_iota(sq) >= _row_iota(sq)).astype(MXU_DTYPE)], axis=0)
    lane = _col_iota((CHUNK, LANES))

    def conv(res, j, w4):
        stage_ref[j] = res[:, j * LANES:(j + 1) * LANES]
        out = None
        for tap in range(w4.shape[0]):
            lo = halo - 2 + tap
            term = stage_ref[j, lo:lo + TOKEN_TILE, :] * w4[tap:tap + 1, :]
            out = term if out is None else out + term
        return out

    def half_silu(hy):
        return hy + hy * jnp.tanh(hy)

    def l2n(x):
        return x * lax.rsqrt(jnp.sum(x * x, axis=-1, keepdims=True) + EPS)

    for s0 in range(0, tm, TOKEN_TILE):
        if s0 == 0:
            before = jnp.where(tile_pos > 0, _rms(rb_ref[...], g_ref[...]), 0.0)
        else:
            before = _rms(r_ref[s0 - halo:s0, :], g_ref[...])
        if s0 + TOKEN_TILE == tm:
            after = jnp.where(tile_pos < tiles_per_seq - 1, _rms(ra_ref[...], g_ref[...]), 0.0)
        else:
            after = _rms(r_ref[s0 + TOKEN_TILE:s0 + TOKEN_TILE + halo, :], g_ref[...])
        h = jnp.concatenate([before, _rms(r_ref[s0:s0 + TOKEN_TILE, :], g_ref[...]), after], axis=0)
        res = _mm(h, w_ref[...])
        rows = slice(s0, s0 + TOKEN_TILE)
        for j in range(N_MAIN_BLK):
            cols = slice(j * LANES, (j + 1) * LANES)
            if j < 3 * DN_HEADS:
                y = half_silu(conv(res, j, 0.5 * dcw_ref[:, cols]))
                if j < DN_HEADS:
                    y = l2n(y) * (DN_HEAD_DIM ** -0.5)
                elif j < 2 * DN_HEADS:
                    y = l2n(y)
            elif 4 * DN_HEADS <= j < 4 * DN_HEADS + LRU_WIDTH // LANES:
                lc = slice((j - 4 * DN_HEADS) * LANES, (j - 4 * DN_HEADS + 1) * LANES)
                y = conv(res, j, lcw_ref[:, lc]) + lcb_ref[:, lc]
            else:
                y = res[halo:halo + TOKEN_TILE, cols]
            proj_ref[j, rows, :] = y.astype(proj_ref.dtype)
        gates = res[halo:halo + TOKEN_TILE, MAIN_COLS:]
        beta = _sigmoid(gates)
        g = -jnp.exp(alog_ref[...]) * jax.nn.softplus(gates + dtb_ref[...])
        for c in range(TOKEN_TILE // CHUNK):
            crow = slice(s0 + c * CHUNK, s0 + (c + 1) * CHUNK)
            gc = g[c * CHUNK:(c + 1) * CHUNK]
            p1 = gc.astype(MXU_DTYPE)
            r1 = gc - p1.astype(F32)
            p2 = r1.astype(MXU_DTYPE)
            p3 = (r1 - p2.astype(F32)).astype(MXU_DTYPE)
            cs = jnp.dot(tri, jnp.concatenate([p1, p2, p3], axis=1), preferred_element_type=F32)
            cs = cs[:, :LANES] + cs[:, LANES:2 * LANES] + cs[:, 2 * LANES:]
            prefix, suffix = cs[:CHUNK], cs[CHUNK:]
            pk = jnp.where(lane < 8, beta[c * CHUNK:(c + 1) * CHUNK],
                           jnp.where(lane < 12, prefix, suffix))
            for hd in range(DN_HEADS):
                pack_ref[hd, crow, :] = pk if hd == 0 else pltpu.roll(pk, LANES - hd, 1)
            pkt = pk.T
            for hd in range(DN_HEADS):
                packt_ref[hd, :, crow] = jnp.concatenate(
                    [pkt[8 + hd:9 + hd], pkt[12 + hd:13 + hd], jnp.zeros((PACKT_ROWS - 2, CHUNK), F32)], axis=0)


def _inproj(r2d, seq, g1, w, alog, dtb, dn_conv_w, lru_conv_w, lru_conv_b):
    t = r2d.shape[0]
    tm = INPROJ_TILE
    hb = tm // SUBLANES
    nhb = t // SUBLANES

    def whole(shape):
        return pl.BlockSpec(shape, lambda i: (0,) * len(shape))

    return pl.pallas_call(
        functools.partial(_inproj_kernel, tiles_per_seq=seq // tm),
        name="inproj",
        grid=(t // tm,),
        in_specs=[
            pl.BlockSpec((SUBLANES, D_MODEL), lambda i: (jnp.maximum(i * hb - 1, 0), 0)),
            pl.BlockSpec((tm, D_MODEL), lambda i: (i, 0)),
            pl.BlockSpec((SUBLANES, D_MODEL), lambda i: (jnp.minimum((i + 1) * hb, nhb - 1), 0)),
            whole((1, D_MODEL)), whole((D_MODEL, PROJ_COLS)), whole((1, LANES)), whole((1, LANES)),
            whole(dn_conv_w.shape), whole(lru_conv_w.shape), whole(lru_conv_b.shape),
        ],
        out_specs=[
            pl.BlockSpec((N_MAIN_BLK, tm, LANES), lambda i: (0, i, 0)),
            pl.BlockSpec((DN_HEADS, tm, LANES), lambda i: (0, i, 0)),
            pl.BlockSpec((DN_HEADS, PACKT_ROWS, tm), lambda i: (0, 0, i)),
        ],
        out_shape=[
            jax.ShapeDtypeStruct((N_MAIN_BLK, t, LANES), MXU_DTYPE),
            jax.ShapeDtypeStruct((DN_HEADS, t, LANES), F32),
            jax.ShapeDtypeStruct((DN_HEADS, PACKT_ROWS, t), F32),
        ],
        scratch_shapes=[pltpu.VMEM((N_MAIN_BLK, TOKEN_TILE + 2 * SUBLANES, LANES), F32)],
        compiler_params=pltpu.CompilerParams(dimension_semantics=("arbitrary",),
                                             vmem_limit_bytes=VMEM_LIMIT),
    )(r2d, r2d, r2d, g1, w, alog, dtb, dn_conv_w, lru_conv_w, lru_conv_b)


N_TRI_MASKS = 6


def _tri_masks():
    i = np.arange(CHUNK)[:, None]
    j = np.arange(CHUNK)[None, :]
    out = np.zeros((2, N_TRI_MASKS, CHUNK, CHUNK), np.float32)
    for d in range(2):
        hi, lo = (j, i) if d == 1 else (i, j)
        out[d, 0] = (hi // SUBLANES == lo // SUBLANES) & (hi > lo)
        s, lvl = SUBLANES, 1
        while s < CHUNK:
            out[d, lvl] = (hi // (2 * s) == lo // (2 * s)) & ((hi // s) % 2 == 1) & ((lo // s) % 2 == 0)
            s, lvl = 2 * s, lvl + 1
        out[d, N_TRI_MASKS - 1] = hi >= lo
    return out


def _unit_tri_inverse_minus_eye(a_list, mask_of, tick, minor):
    idx = range(len(a_list))

    def stage(fn):
        out = []
        for i in idx:
            out.append(fn(i))
            minor()
        tick()
        return out

    x = [a_list[i] * mask_of(i, 0) for i in idx]
    y2 = stage(lambda i: _mm(x[i], x[i]))
    y4 = stage(lambda i: _mm(y2[i], y2[i]))
    xy2 = stage(lambda i: _mm(x[i], y2[i]))
    m1 = [y2[i] - x[i].astype(F32) - xy2[i] for i in idx]
    m1y4 = stage(lambda i: _mm(m1[i], y4[i]))
    n = [m1[i] + y4[i] + m1y4[i] for i in idx]
    for lvl in range(1, N_TRI_MASKS - 1):
        b = [a_list[i] * mask_of(i, lvl) for i in idx]
        p = stage(lambda i: b[i].astype(F32) + _mm(n[i], b[i]))
        pn = stage(lambda i: _mm(p[i], n[i]))
        n = [n[i] - p[i] - pn[i] for i in idx]
    return n


def _seqmix_kernel(q_ref, k_ref, v_ref, pack_ref, packt_ref, z_ref, ng_ref, mask_ref,
                   lx_ref, lg_ref, wg_ref, bg_ref, lam_ref, out_ref, y_ref,
                   c_ref, nmq_ref, cd_ref, o_ref, h_ref, *lvl):
    t = pl.program_id(0)
    cur = t % 2
    prv = 1 - cur
    seq = q_ref.shape[1]
    nc = seq // CHUNK

    @pl.when(t == 0)
    def _():
        c_ref[1] = jnp.zeros(c_ref.shape[1:], c_ref.dtype)
        nmq_ref[1] = jnp.zeros(nmq_ref.shape[1:], nmq_ref.dtype)
        cd_ref[1] = jnp.zeros(cd_ref.shape[1:], cd_ref.dtype)
        o_ref[1] = jnp.zeros(o_ref.shape[1:], o_ref.dtype)

    o_ref[cur] = jnp.zeros(o_ref.shape[1:], o_ref.dtype)

    sq = (CHUNK, CHUNK)
    eye = (_row_iota(sq) == _col_iota(sq)).astype(MXU_DTYPE)

    groups = nc // PHASE_A_GROUP

    def kq_stage(g):
        out = []
        for n in range(g * PHASE_A_GROUP, (g + 1) * PHASE_A_GROUP):
            rows = slice(n * CHUNK, (n + 1) * CHUNK)
            qc, kc = q_ref[0, rows, :], k_ref[0, rows, :]
            out.append((n, rows, _mm_nt(jnp.concatenate([kc, qc, eye], axis=0), kc)))
        return out

    def prep(n, rows, kq, d):
        pk = pack_ref[0, rows, :]
        qc = q_ref[0, rows, :].astype(F32)
        kc = k_ref[0, rows, :].astype(F32)
        vc = v_ref[0, rows, :].astype(F32)
        beta = jnp.broadcast_to(pk[:, 4 * d:4 * d + 1], sq)
        gc = jnp.broadcast_to(pk[:, 8 + 4 * d:9 + 4 * d], sq)
        gc_row = packt_ref[0, d:d + 1, rows]
        glast = gc[0:1, :] if d == 1 else gc[CHUNK - 1:CHUNK, :]
        e_in = jnp.exp(gc)
        decay = jnp.exp(jnp.minimum(gc - gc_row, 0.0)) * mask_ref[d, N_TRI_MASKS - 1].astype(F32)
        attn = (kq[CHUNK:2 * CHUNK] * decay).astype(MXU_DTYPE)
        kdt = (kq[2 * CHUNK:] * jnp.exp(glast - gc_row)).astype(MXU_DTYPE)
        cd_ref[cur, d, n] = jnp.exp(jnp.broadcast_to(glast, (SUBLANES, LANES)))
        rhs = jnp.concatenate([vc * beta, kc * (beta * e_in)], axis=1).astype(MXU_DTYPE)
        a_raw = (beta * kq[:CHUNK] * decay).astype(MXU_DTYPE)
        return (d, n, rows, a_raw, rhs, attn, kdt, qc * e_in)

    def b_step(i, states):
        ns = (i, nc - 1 - i)
        rs = [_mm(nmq_ref[prv, d, ns[d]], states[d]) for d in range(2)]
        for d in range(2):
            o_ref[prv, ns[d] * CHUNK:(ns[d] + 1) * CHUNK, :] += rs[d][CHUNK:]
        return [states[d] * cd_ref[prv, d, ns[d]][0:1, :] + rs[d][:CHUNK] + c_ref[prv, d, ns[d]]
                for d in range(2)]

    zero = jnp.zeros((DN_HEAD_DIM, DN_HEAD_DIM), F32)
    box = {"states": [zero, zero], "b_next": 0, "minor": 0}
    lru = _lru_steps(lx_ref, lg_ref, wg_ref, bg_ref, lam_ref, y_ref, h_ref, lvl)
    inst = [prep(n, rows, kq, d) for (n, rows, kq) in kq_stage(0) for d in range(2)]
    for g in range(groups):
        pending = ([(n, rows, kq, d) for (n, rows, kq) in kq_stage(g + 1) for d in range(2)]
                   if g + 1 < groups else [])
        nxt = []
        b_end = (g + 1) * PHASE_A_GROUP

        def tick():
            if box["b_next"] < b_end:
                box["states"] = b_step(box["b_next"], box["states"])
                box["b_next"] += 1
            for _ in range(2):
                if pending:
                    nxt.append(prep(*pending.pop(0)))

        def minor():
            box["minor"] += 1
            if box["minor"] % LRU_PIECE_EVERY == 0:
                next(lru, None)

        idx = range(len(inst))
        n_inv = _unit_tri_inverse_minus_eye([v[3] for v in inst], lambda i, m: mask_ref[inst[i][0], m],
                                            tick, minor)
        while pending:
            nxt.append(prep(*pending.pop(0)))
        assert box["b_next"] == b_end
        uw, kuw, auw = [], [], []
        for i in idx:
            uw.append((inst[i][4].astype(F32) + _mm(n_inv[i], inst[i][4])).astype(MXU_DTYPE))
            minor()
        for i in idx:
            kuw.append(_mm(inst[i][6], uw[i]))
            auw.append(_mm(inst[i][5], uw[i]))
            minor()
        for i in idx:
            d, n, rows = inst[i][:3]
            c_ref[cur, d, n] = kuw[i][:, :DN_HEAD_DIM]
            nmq_ref[cur, d, n, :CHUNK, :] = (-kuw[i][:, DN_HEAD_DIM:]).astype(nmq_ref.dtype)
            nmq_ref[cur, d, n, CHUNK:, :] = (inst[i][7] - auw[i][:, DN_HEAD_DIM:]).astype(nmq_ref.dtype)
            o_ref[cur, rows, :] += auw[i][:, :DN_HEAD_DIM]
        inst = nxt
    for _ in lru:
        pass

    z = z_ref[0].astype(F32)
    out_ref[0] = (_rms(o_ref[prv], ng_ref[...]) * (z * _sigmoid(z))).astype(out_ref.dtype)


def _scan_level_sizes(seq):
    sizes = [seq]
    while sizes[-1] > SCAN_BOTTOM_ROWS:
        assert sizes[-1] % (SCAN_FANOUT * SUBLANES) == 0
        sizes.append(sizes[-1] // SCAN_FANOUT)
    return sizes


def _lru_steps(lx_ref, lg_ref, wg_ref, bg_ref, lam_ref, y_ref, h_ref, lvl):
    nlev = (len(lvl) + 1) // 2
    ab = lvl[:nlev]
    cs = (None,) + tuple(lvl[nlev:])
    seq = lx_ref.shape[1]
    fan = SCAN_FANOUT
    sizes = [r.shape[2] for r in ab]

    for r0 in range(0, seq, LRU_PIECE_ROWS):
        rows = slice(r0, r0 + LRU_PIECE_ROWS)
        xc = lx_ref[0, rows, :].astype(F32)
        gates = _mm(xc, wg_ref[0]) + bg_ref[0]
        for d in range(2):
            r = _sigmoid(gates[:, (2 * d) * LANES:(2 * d + 1) * LANES])
            ig = _sigmoid(gates[:, (2 * d + 1) * LANES:(2 * d + 2) * LANES])
            log_a = -LRU_C * r * jax.nn.softplus(-lam_ref[0, d:d + 1, :])
            a = jnp.exp(log_a)
            ab[0][d, 0, rows, :] = a
            v = -jnp.tanh(log_a) * (a * a + 1.0)
            ab[0][d, 1, rows, :] = jnp.where(v > 0.0, v * lax.rsqrt(v), 0.0) * (ig * xc)
        yield
    for d in range(2):
        for l in range(1, nlev):
            cs[l][d, 0:SUBLANES, :] = jnp.zeros((SUBLANES, LANES), F32)
            cs[l][d, SUBLANES + sizes[l]:, :] = jnp.zeros((SUBLANES, LANES), F32)

    def slab(j, r0, n):
        return pl.ds(j + fan * r0, n, stride=fan)

    for l in range(nlev - 1):
        per_slab = sizes[l] // fan
        n = min(per_slab, LRU_PIECE_ROWS)
        for d in range(2):
            order = list(range(fan)) if d == 0 else list(range(fan - 1, -1, -1))
            for r0 in range(0, per_slab, n):
                p = ab[l][d, 0, slab(order[0], r0, n), :]
                h = ab[l][d, 1, slab(order[0], r0, n), :]
                for j in order[1:]:
                    aj = ab[l][d, 0, slab(j, r0, n), :]
                    h = aj * h + ab[l][d, 1, slab(j, r0, n), :]
                    p = aj * p
                    ab[l][d, 0, slab(j, r0, n), :] = p
                    ab[l][d, 1, slab(j, r0, n), :] = h
                ab[l + 1][d, 0, r0:r0 + n, :] = p
                ab[l + 1][d, 1, r0:r0 + n, :] = h
                yield

    top = nlev - 1
    rows = sizes[top]
    row = _row_iota((rows, LANES))
    for d in range(2):
        a = ab[top][d, 0]
        b = ab[top][d, 1]
        s = 1
        while s < rows:
            sh = rows - s if d == 1 else s
            keep = (row < rows - s) if d == 1 else (row >= s)
            b = jnp.where(keep, a * pltpu.roll(b, sh, 0) + b, b)
            a = jnp.where(keep, a * pltpu.roll(a, sh, 0), a)
            s *= 2
        if top == 0:
            ab[0][d, 1] = b
        else:
            cs[top][d, SUBLANES:SUBLANES + rows, :] = b
    yield

    for l in range(nlev - 2, -1, -1):
        per_slab = sizes[l] // fan
        n = min(per_slab, LRU_PIECE_ROWS)
        for r0 in range(0, per_slab, n):
            carry = [cs[l + 1][d, SUBLANES - 1 + 2 * d + r0:SUBLANES - 1 + 2 * d + r0 + n, :] for d in range(2)]
            for j in range(fan):
                true = [ab[l][d, 1, slab(j, r0, n), :] + ab[l][d, 0, slab(j, r0, n), :] * carry[d]
                        for d in range(2)]
                if l == 0:
                    h_ref[slab(j, r0, n), :] = true[0] + true[1]
                else:
                    for d in range(2):
                        cs[l][d, pl.ds(SUBLANES + j + fan * r0, n, stride=fan), :] = true[d]
            yield
    if nlev == 1:
        h_ref[...] = ab[0][0, 1] + ab[0][1, 1]
    for r0 in range(0, seq, LRU_PIECE_ROWS):
        rows = slice(r0, r0 + LRU_PIECE_ROWS)
        y_ref[0, rows, :] = (jax.nn.gelu(lg_ref[0, rows, :].astype(F32)) * h_ref[rows, :]).astype(y_ref.dtype)
        yield


def _seqmix(proj, pack, packt, norm_g, wgate, bgate, lam, batch, seq):
    nc = seq // CHUNK
    assert nc % PHASE_A_GROUP == 0
    blk = (1, seq, LANES)
    masks = jnp.asarray(_tri_masks(), MXU_DTYPE)
    items = batch * DN_HEADS
    nblk = LRU_WIDTH // LANES
    assert nblk == DN_HEADS
    j_lx = 4 * DN_HEADS
    j_lg = j_lx + nblk
    sizes = _scan_level_sizes(seq)

    def cur_item(t):
        i = jnp.minimum(t, items - 1)
        return i % DN_HEADS, i // DN_HEADS

    def prev_item(t):
        i = jnp.maximum(t - 1, 0)
        return i % DN_HEADS, i // DN_HEADS

    def col(j0):
        return pl.BlockSpec(blk, lambda t: (j0 + cur_item(t)[0], cur_item(t)[1], 0))

    def per_block(shape):
        return pl.BlockSpec(shape, lambda t: (cur_item(t)[0],) + (0,) * (len(shape) - 1))

    return pl.pallas_call(
        _seqmix_kernel,
        name="seqmix",
        grid=(items + 1,),
        in_specs=[col(0), col(DN_HEADS), col(2 * DN_HEADS),
                  pl.BlockSpec(blk, lambda t: (cur_item(t)[0], cur_item(t)[1], 0)),
                  pl.BlockSpec((1, PACKT_ROWS, seq), lambda t: (cur_item(t)[0], 0, cur_item(t)[1])),
                  pl.BlockSpec(blk, lambda t: (3 * DN_HEADS + prev_item(t)[0], prev_item(t)[1], 0)),
                  pl.BlockSpec((1, LANES), lambda t: (0, 0)),
                  pl.BlockSpec(masks.shape, lambda t: (0, 0, 0, 0)),
                  col(j_lx), col(j_lg),
                  per_block((1, LANES, 4 * LANES)), per_block((1, 1, 4 * LANES)), per_block((1, 2, LANES))],
        out_specs=[pl.BlockSpec(blk, lambda t: (prev_item(t)[0], prev_item(t)[1], 0)),
                   pl.BlockSpec(blk, lambda t: (cur_item(t)[0], cur_item(t)[1], 0))],
        out_shape=[jax.ShapeDtypeStruct((DN_HEADS, batch * seq, LANES), MXU_DTYPE),
                   jax.ShapeDtypeStruct((nblk, batch * seq, LANES), MXU_DTYPE)],
        scratch_shapes=(
            [pltpu.VMEM((2, 2, nc, DN_HEAD_DIM, DN_HEAD_DIM), F32),
             pltpu.VMEM((2, 2, nc, 2 * CHUNK, LANES), MXU_DTYPE),
             pltpu.VMEM((2, 2, nc, SUBLANES, LANES), F32),
             pltpu.VMEM((2, seq, LANES), F32),
             pltpu.VMEM((seq, LANES), F32)]
            + [pltpu.VMEM((2, 2, m, LANES), F32) for m in sizes]
            + [pltpu.VMEM((2, m + 2 * SUBLANES, LANES), F32) for m in sizes[1:]]),
        compiler_params=pltpu.CompilerParams(dimension_semantics=("arbitrary",),
                                             vmem_limit_bytes=VMEM_LIMIT),
    )(proj, proj, proj, pack, packt, proj, norm_g, masks, proj, proj, wgate, bgate, lam)


def _mixer_kernel(rp_ref, rm_ref, rn_ref, dnp_ref, dnm_ref, dnn_ref, lyp_ref, lym_ref, lyn_ref, p_ref,
                  lng_ref, wo_ref, g2_ref, wg_ref, wu_ref, cw_ref, cb_ref, wd_ref,
                  png_ref, pwg_ref, pbg_ref, pwp_ref, fg_ref, out_ref, mix_ref, hcat_ref, *, final):
    t = pl.program_id(1)
    tm = rm_ref.shape[1]
    halo = BF16_ROWS
    segments = ((0, halo, rp_ref, dnp_ref, lyp_ref), (halo, tm, rm_ref, dnm_ref, lym_ref),
                (halo + tm, halo, rn_ref, dnn_ref, lyn_ref))
    for lo, n, _, dn_ref, ly_ref in segments:
        for j in range(DN_HEADS):
            mix_ref[lo:lo + n, j * LANES:(j + 1) * LANES] = dn_ref[j]
        ly = jnp.concatenate([ly_ref[j] for j in range(ly_ref.shape[0])], axis=1).astype(F32)
        mix_ref[lo:lo + n, DN_WIDTH:] = _rms(ly, lng_ref[...]).astype(mix_ref.dtype)
    proj = jnp.dot(mix_ref[...], wo_ref[...], preferred_element_type=F32)
    r1 = rm_ref[0] + proj[halo:halo + tm]
    hcat_ref[halo:halo + tm, :] = _rms(r1, g2_ref[...]).astype(hcat_ref.dtype)
    h_prev = _rms(rp_ref[0] + proj[:halo], g2_ref[...])
    h_next = _rms(rn_ref[0] + proj[halo + tm:], g2_ref[...])
    hcat_ref[0:halo, :] = jnp.where(t > 0, h_prev, 0.0).astype(hcat_ref.dtype)
    hcat_ref[halo + tm:, :] = jnp.where(t < pl.num_programs(1) - 1, h_next, 0.0).astype(hcat_ref.dtype)

    rows = tm + 2 * halo
    acc = r1
    for c, width in enumerate(FF_CHUNKS):
        cs = slice(sum(FF_CHUNKS[:c]), sum(FF_CHUNKS[:c]) + width)
        gfull = jnp.dot(hcat_ref[...], wg_ref[:, cs], preferred_element_type=F32)
        gate = (pltpu.roll(gfull, 1, 0)[halo:halo + tm] * cw_ref[0:1, cs]
                + gfull[halo:halo + tm] * cw_ref[1:2, cs]
                + pltpu.roll(gfull, rows - 1, 0)[halo:halo + tm] * cw_ref[2:3, cs]
                + cb_ref[:, cs])
        up = jnp.dot(hcat_ref[halo:halo + tm, :], wu_ref[:, cs], preferred_element_type=F32)
        acc = acc + _mm(jax.nn.gelu(gate) * up, wd_ref[cs, :])
    pgate = _sigmoid(_mm(_rms(acc, png_ref[...]), pwg_ref[...]) + pbg_ref[...])
    res = acc + pgate * _mm(p_ref[0, 0], pwp_ref[...])
    out_ref[0] = _rms(res, fg_ref[...]) if final else res


def _mixer(r, dn, ly, p, layer, final, lng, wo, g2, wg, wu, cw, cb, wd, png, pwg, pbg, pwp, fg):
    batch, seq, _ = r.shape
    tm = TOKEN_TILE
    nt = seq // tm
    hb = tm // BF16_ROWS
    nhb = seq // BF16_ROWS
    nb = LRU_WIDTH // LANES

    def prev_blk(t):
        return jnp.maximum(t * hb - 1, 0)

    def next_blk(t):
        return jnp.minimum((t + 1) * hb, nhb - 1)

    def whole(shape):
        return pl.BlockSpec(shape, lambda b, t: (0,) * len(shape))

    def tok3(nlead):
        return [pl.BlockSpec((nlead, BF16_ROWS, LANES), lambda b, t: (0, b * nhb + prev_blk(t), 0)),
                pl.BlockSpec((nlead, tm, LANES), lambda b, t: (0, b * nt + t, 0)),
                pl.BlockSpec((nlead, BF16_ROWS, LANES), lambda b, t: (0, b * nhb + next_blk(t), 0))]

    return pl.pallas_call(
        functools.partial(_mixer_kernel, final=final),
        name="mixer",
        grid=(batch, nt),
        in_specs=[
            pl.BlockSpec((1, BF16_ROWS, D_MODEL), lambda b, t: (b, prev_blk(t), 0)),
            pl.BlockSpec((1, tm, D_MODEL), lambda b, t: (b, t, 0)),
            pl.BlockSpec((1, BF16_ROWS, D_MODEL), lambda b, t: (b, next_blk(t), 0)),
            *tok3(DN_HEADS), *tok3(nb),
            pl.BlockSpec((1, 1, tm, PLE_DIM), lambda b, t: (layer, b, t, 0)),
            whole((1, LRU_WIDTH)), whole((D_MODEL, D_MODEL)), whole((1, D_MODEL)),
            whole((D_MODEL, D_FF)), whole((D_MODEL, D_FF)), whole((3, D_FF)), whole((1, D_FF)),
            whole((D_FF, D_MODEL)), whole((1, D_MODEL)), whole((D_MODEL, D_MODEL)),
            whole((1, D_MODEL)), whole((PLE_DIM, D_MODEL)), whole((1, D_MODEL)),
        ],
        out_specs=pl.BlockSpec((1, tm, D_MODEL), lambda b, t: (b, t, 0)),
        out_shape=jax.ShapeDtypeStruct((batch, seq, D_MODEL), F32),
        scratch_shapes=[pltpu.VMEM((tm + 2 * BF16_ROWS, D_MODEL), MXU_DTYPE),
                        pltpu.VMEM((tm + 2 * BF16_ROWS, D_MODEL), MXU_DTYPE)],
        compiler_params=pltpu.CompilerParams(dimension_semantics=("arbitrary", "arbitrary"),
                                             vmem_limit_bytes=VMEM_LIMIT),
    )(r, r, r, dn, dn, dn, ly, ly, ly, p, lng, wo, g2, wg, wu, cw, cb, wd, png, pwg, pbg, pwp, fg)


def _row(v):
    return v.reshape(1, -1).astype(F32)


def _gate_lane_row(v):
    return jnp.zeros((1, LANES), F32).at[0, 8:16].set(v.reshape(-1).astype(F32))


def _lru_gate_weights(wa, ba, wx, bx):
    nblk = LRU_WIDTH // LANES
    per = LANES // LRU_BLOCK

    def bd(w):
        w = w.reshape(nblk, per, LRU_BLOCK, LRU_BLOCK)
        eye = jnp.eye(per, dtype=w.dtype)
        return jnp.einsum('npcd,pq->npcqd', w, eye).reshape(nblk, LANES, LANES)

    wcat = jnp.concatenate([bd(wa[0]), bd(wx[0]), bd(wa[1]), bd(wx[1])], axis=2)
    bcat = jnp.concatenate([ba[0].reshape(nblk, 1, LANES), bx[0].reshape(nblk, 1, LANES),
                            ba[1].reshape(nblk, 1, LANES), bx[1].reshape(nblk, 1, LANES)], axis=2)
    return wcat.astype(MXU_DTYPE), bcat.astype(F32)


def kernel(x, p, norm1_g, w_in, dn_conv_w, dn_a_log, dn_dt_bias, dn_norm_g, lru_conv_w, lru_conv_b,
           lru_wa, lru_ba, lru_wx, lru_bx, lru_lambda, lru_norm_g, w_out, norm2_g, ffn_wg, ffn_wu,
           ffn_conv_w, ffn_conv_b, ffn_wd, ple_norm_g, ple_wg, ple_bg, ple_wp, final_g):
    batch, seq, _ = x.shape
    depth = w_in.shape[0]
    assert seq % INPROJ_TILE == 0 and seq % CHUNK == 0
    nblk = LRU_WIDTH // LANES
    r = x
    for i in range(depth):
        w_perm = jnp.concatenate(
            [w_in[i][:, :BETA_OFF], w_in[i][:, LX_OFF:], w_in[i][:, BETA_OFF:LX_OFF],
             jnp.zeros((D_MODEL, LANES - (LX_OFF - BETA_OFF)), w_in.dtype)], axis=1).astype(MXU_DTYPE)
        proj, pack, packt = _inproj(r.reshape(batch * seq, D_MODEL), seq, _row(norm1_g[i]), w_perm,
                                    _gate_lane_row(dn_a_log[i]), _gate_lane_row(dn_dt_bias[i]),
                                    dn_conv_w[i], lru_conv_w[i], _row(lru_conv_b[i]))
        wgate, bgate = _lru_gate_weights(lru_wa[i], lru_ba[i], lru_wx[i], lru_bx[i])
        lam = jnp.transpose(lru_lambda[i].reshape(2, nblk, LANES), (1, 0, 2)).astype(F32)
        dn, ly = _seqmix(proj, pack, packt, _row(dn_norm_g[i]), wgate, bgate, lam, batch, seq)
        r = _mixer(r, dn, ly, p, i, i == depth - 1, _row(lru_norm_g[i]), w_out[i].astype(MXU_DTYPE),
                   _row(norm2_g[i]), ffn_wg[i].astype(MXU_DTYPE), ffn_wu[i].astype(MXU_DTYPE),
                   ffn_conv_w[i], _row(ffn_conv_b[i]), ffn_wd[i].astype(MXU_DTYPE), _row(ple_norm_g[i]),
                   ple_wg[i].astype(MXU_DTYPE), _row(ple_bg[i]), ple_wp[i].astype(MXU_DTYPE), _row(final_g))
    return r
```

```python
import functools

import jax
import numpy as np
import jax.numpy as jnp
from jax import lax
from jax.experimental import pallas as pl
from jax.experimental.pallas import tpu as pltpu

D_MODEL = 1024
DN_HEADS = 4
DN_HEAD_DIM = 128
DN_WIDTH = DN_HEADS * DN_HEAD_DIM
LRU_WIDTH = 512
LRU_BLOCKS = 8
LRU_BLOCK = LRU_WIDTH // LRU_BLOCKS
LRU_C = 8.0
D_FF = 2816
PLE_DIM = 256
EPS = 1e-6
Z_OFF = 3 * DN_WIDTH
BETA_OFF = 4 * DN_WIDTH
LX_OFF = BETA_OFF + 4 * DN_HEADS
IN_COLS = LX_OFF + 2 * LRU_WIDTH

LANES = 128
SUBLANES = 8
BF16_ROWS = 16
CHUNK = 128
PACKT_ROWS = 8
SCAN_FANOUT = 4
SCAN_BOTTOM_ROWS = 32
LRU_PIECE_ROWS = 64
LRU_PIECE_EVERY = 4
PHASE_A_GROUP = 8
MAIN_COLS = 4 * DN_WIDTH + 2 * LRU_WIDTH
N_MAIN_BLK = MAIN_COLS // LANES
PROJ_COLS = MAIN_COLS + LANES
TOKEN_TILE = 512
INPROJ_TILE = 2 * TOKEN_TILE
MXU_TILE = 256
FF_CHUNKS = (6 * MXU_TILE, 5 * MXU_TILE)
assert sum(FF_CHUNKS) == D_FF
VMEM_LIMIT = 56 * 1024 * 1024

MXU_DTYPE = jnp.bfloat16
F32 = jnp.float32


def _mm(a, b):
    return jnp.dot(a.astype(MXU_DTYPE), b.astype(MXU_DTYPE), preferred_element_type=F32)


def _mm_nt(a, b):
    return lax.dot_general(a.astype(MXU_DTYPE), b.astype(MXU_DTYPE), (((1,), (1,)), ((), ())),
                           preferred_element_type=F32)


def _rms(x, g):
    return x * lax.rsqrt(jnp.mean(x * x, axis=-1, keepdims=True) + EPS) * g


def _row_iota(shape):
    return lax.broadcasted_iota(jnp.int32, shape, 0)


def _col_iota(shape):
    return lax.broadcasted_iota(jnp.int32, shape, 1)


def _sigmoid(x):
    return 0.5 * jnp.tanh(0.5 * x) + 0.5


def _inproj_kernel(rb_ref, r_ref, ra_ref, g_ref, w_ref, alog_ref, dtb_ref, dcw_ref, lcw_ref, lcb_ref,
                   proj_ref, pack_ref, packt_ref, stage_ref, *, tiles_per_seq):
    tile_pos = pl.program_id(0) % tiles_per_seq
    tm = r_ref.shape[0]
    halo = SUBLANES
    sq = (CHUNK, CHUNK)
    tri = jnp.concatenate([(_col_iota(sq) <= _row_iota(sq)).astype(MXU_DTYPE),
                           (_col_iota(sq) >= _row_iota(sq)).astype(MXU_DTYPE)], axis=0)
    lane = _col_iota((CHUNK, LANES))

    def conv(res, j, w4):
        stage_ref[j] = res[:, j * LANES:(j + 1) * LANES]
        out = None
        for tap in range(w4.shape[0]):
            lo = halo - 2 + tap
            term = stage_ref[j, lo:lo + TOKEN_TILE, :] * w4[tap:tap + 1, :]
            out = term if out is None else out + term
        return out

    def half_silu(hy):
        return hy + hy * jnp.tanh(hy)

    def l2n(x):
        return x * lax.rsqrt(jnp.sum(x * x, axis=-1, keepdims=True) + EPS)

    for s0 in range(0, tm, TOKEN_TILE):
        if s0 == 0:
            before = jnp.where(tile_pos > 0, _rms(rb_ref[...], g_ref[...]), 0.0)
        else:
            before = _rms(r_ref[s0 - halo:s0, :], g_ref[...])
        if s0 + TOKEN_TILE == tm:
            after = jnp.where(tile_pos < tiles_per_seq - 1, _rms(ra_ref[...], g_ref[...]), 0.0)
        else:
            after = _rms(r_ref[s0 + TOKEN_TILE:s0 + TOKEN_TILE + halo, :], g_ref[...])
        h = jnp.concatenate([before, _rms(r_ref[s0:s0 + TOKEN_TILE, :], g_ref[...]), after], axis=0)
        res = _mm(h, w_ref[...])
        rows = slice(s0, s0 + TOKEN_TILE)
        for j in range(N_MAIN_BLK):
            cols = slice(j * LANES, (j + 1) * LANES)
            if j < 3 * DN_HEADS:
                y = half_silu(conv(res, j, 0.5 * dcw_ref[:, cols]))
                if j < DN_HEADS:
                    y = l2n(y) * (DN_HEAD_DIM ** -0.5)
                elif j < 2 * DN_HEADS:
                    y = l2n(y)
            elif 4 * DN_HEADS <= j < 4 * DN_HEADS + LRU_WIDTH // LANES:
                lc = slice((j - 4 * DN_HEADS) * LANES, (j - 4 * DN_HEADS + 1) * LANES)
                y = conv(res, j, lcw_ref[:, lc]) + lcb_ref[:, lc]
            else:
                y = res[halo:halo + TOKEN_TILE, cols]
            proj_ref[j, rows, :] = y.astype(proj_ref.dtype)
        gates = res[halo:halo + TOKEN_TILE, MAIN_COLS:]
        beta = _sigmoid(gates)
        g = -jnp.exp(alog_ref[...]) * jax.nn.softplus(gates + dtb_ref[...])
        for c in range(TOKEN_TILE // CHUNK):
            crow = slice(s0 + c * CHUNK, s0 + (c + 1) * CHUNK)
            gc = g[c * CHUNK:(c + 1) * CHUNK]
            p1 = gc.astype(MXU_DTYPE)
            r1 = gc - p1.astype(F32)
            p2 = r1.astype(MXU_DTYPE)
            p3 = (r1 - p2.astype(F32)).astype(MXU_DTYPE)
            cs = jnp.dot(tri, jnp.concatenate([p1, p2, p3], axis=1), preferred_element_type=F32)
            cs = cs[:, :LANES] + cs[:, LANES:2 * LANES] + cs[:, 2 * LANES:]
            prefix, suffix = cs[:CHUNK], cs[CHUNK:]
            pk = jnp.where(lane < 8, beta[c * CHUNK:(c + 1) * CHUNK],
                           jnp.where(lane < 12, prefix, suffix))
            for hd in range(DN_HEADS):
                pack_ref[hd, crow, :] = pk if hd == 0 else pltpu.roll(pk, LANES - hd, 1)
            pkt = pk.T
            for hd in range(DN_HEADS):
                packt_ref[hd, :, crow] = jnp.concatenate(
                    [pkt[8 + hd:9 + hd], pkt[12 + hd:13 + hd], jnp.zeros((PACKT_ROWS - 2, CHUNK), F32)], axis=0)


def _inproj(r2d, seq, g1, w, alog, dtb, dn_conv_w, lru_conv_w, lru_conv_b):
    t = r2d.shape[0]
    tm = INPROJ_TILE
    hb = tm // SUBLANES
    nhb = t // SUBLANES

    def whole(shape):
        return pl.BlockSpec(shape, lambda i: (0,) * len(shape))

    return pl.pallas_call(
        functools.partial(_inproj_kernel, tiles_per_seq=seq // tm),
        name="inproj",
        grid=(t // tm,),
        in_specs=[
            pl.BlockSpec((SUBLANES, D_MODEL), lambda i: (jnp.maximum(i * hb - 1, 0), 0)),
            pl.BlockSpec((tm, D_MODEL), lambda i: (i, 0)),
            pl.BlockSpec((SUBLANES, D_MODEL), lambda i: (jnp.minimum((i + 1) * hb, nhb - 1), 0)),
            whole((1, D_MODEL)), whole((D_MODEL, PROJ_COLS)), whole((1, LANES)), whole((1, LANES)),
            whole(dn_conv_w.shape), whole(lru_conv_w.shape), whole(lru_conv_b.shape),
        ],
        out_specs=[
            pl.BlockSpec((N_MAIN_BLK, tm, LANES), lambda i: (0, i, 0)),
            pl.BlockSpec((DN_HEADS, tm, LANES), lambda i: (0, i, 0)),
            pl.BlockSpec((DN_HEADS, PACKT_ROWS, tm), lambda i: (0, 0, i)),
        ],
        out_shape=[
            jax.ShapeDtypeStruct((N_MAIN_BLK, t, LANES), MXU_DTYPE),
            jax.ShapeDtypeStruct((DN_HEADS, t, LANES), F32),
            jax.ShapeDtypeStruct((DN_HEADS, PACKT_ROWS, t), F32),
        ],
        scratch_shapes=[pltpu.VMEM((N_MAIN_BLK, TOKEN_TILE + 2 * SUBLANES, LANES), F32)],
        compiler_params=pltpu.CompilerParams(dimension_semantics=("arbitrary",),
                                             vmem_limit_bytes=VMEM_LIMIT),
    )(r2d, r2d, r2d, g1, w, alog, dtb, dn_conv_w, lru_conv_w, lru_conv_b)


N_TRI_MASKS = 6


def _tri_masks():
    i = np.arange(CHUNK)[:, None]
    j = np.arange(CHUNK)[None, :]
    out = np.zeros((2, N_TRI_MASKS, CHUNK, CHUNK), np.float32)
    for d in range(2):
        hi, lo = (j, i) if d == 1 else (i, j)
        out[d, 0] = (hi // SUBLANES == lo // SUBLANES) & (hi > lo)
        s, lvl = SUBLANES, 1
        while s < CHUNK:
            out[d, lvl] = (hi // (2 * s) == lo // (2 * s)) & ((hi // s) % 2 == 1) & ((lo // s) % 2 == 0)
            s, lvl = 2 * s, lvl + 1
        out[d, N_TRI_MASKS - 1] = hi >= lo
    return out


def _unit_tri_inverse_minus_eye(a_list, mask_of, tick, minor):
    idx = range(len(a_list))

    def stage(fn):
        out = []
        for i in idx:
            out.append(fn(i))
            minor()
        tick()
        return out

    x = [a_list[i] * mask_of(i, 0) for i in idx]
    y2 = stage(lambda i: _mm(x[i], x[i]))
    y4 = stage(lambda i: _mm(y2[i], y2[i]))
    xy2 = stage(lambda i: _mm(x[i], y2[i]))
    m1 = [y2[i] - x[i].astype(F32) - xy2[i] for i in idx]
    m1y4 = stage(lambda i: _mm(m1[i], y4[i]))
    n = [m1[i] + y4[i] + m1y4[i] for i in idx]
    for lvl in range(1, N_TRI_MASKS - 1):
        b = [a_list[i] * mask_of(i, lvl) for i in idx]
        p = stage(lambda i: b[i].astype(F32) + _mm(n[i], b[i]))
        pn = stage(lambda i: _mm(p[i], n[i]))
        n = [n[i] - p[i] - pn[i] for i in idx]
    return n


def _seqmix_kernel(q_ref, k_ref, v_ref, pack_ref, packt_ref, z_ref, ng_ref, mask_ref,
                   lx_ref, lg_ref, wg_ref, bg_ref, lam_ref, out_ref, y_ref,
                   c_ref, nmq_ref, cd_ref, o_ref, h_ref, *lvl):
    t = pl.program_id(0)
    cur = t % 2
    prv = 1 - cur
    seq = q_ref.shape[1]
    nc = seq // CHUNK

    @pl.when(t == 0)
    def _():
        c_ref[1] = jnp.zeros(c_ref.shape[1:], c_ref.dtype)
        nmq_ref[1] = jnp.zeros(nmq_ref.shape[1:], nmq_ref.dtype)
        cd_ref[1] = jnp.zeros(cd_ref.shape[1:], cd_ref.dtype)
        o_ref[1] = jnp.zeros(o_ref.shape[1:], o_ref.dtype)

    o_ref[cur] = jnp.zeros(o_ref.shape[1:], o_ref.dtype)

    sq = (CHUNK, CHUNK)
    eye = (_row_iota(sq) == _col_iota(sq)).astype(MXU_DTYPE)

    groups = nc // PHASE_A_GROUP

    def kq_stage(g):
        out = []
        for n in range(g * PHASE_A_GROUP, (g + 1) * PHASE_A_GROUP):
            rows = slice(n * CHUNK, (n + 1) * CHUNK)
            qc, kc = q_ref[0, rows, :], k_ref[0, rows, :]
            out.append((n, rows, _mm_nt(jnp.concatenate([kc, qc, eye], axis=0), kc)))
        return out

    def prep(n, rows, kq, d):
        pk = pack_ref[0, rows, :]
        qc = q_ref[0, rows, :].astype(F32)
        kc = k_ref[0, rows, :].astype(F32)
        vc = v_ref[0, rows, :].astype(F32)
        beta = jnp.broadcast_to(pk[:, 4 * d:4 * d + 1], sq)
        gc = jnp.broadcast_to(pk[:, 8 + 4 * d:9 + 4 * d], sq)
        gc_row = packt_ref[0, d:d + 1, rows]
        glast = gc[0:1, :] if d == 1 else gc[CHUNK - 1:CHUNK, :]
        e_in = jnp.exp(gc)
        decay = jnp.exp(jnp.minimum(gc - gc_row, 0.0)) * mask_ref[d, N_TRI_MASKS - 1].astype(F32)
        attn = (kq[CHUNK:2 * CHUNK] * decay).astype(MXU_DTYPE)
        kdt = (kq[2 * CHUNK:] * jnp.exp(glast - gc_row)).astype(MXU_DTYPE)
        cd_ref[cur, d, n] = jnp.exp(jnp.broadcast_to(glast, (SUBLANES, LANES)))
        rhs = jnp.concatenate([vc * beta, kc * (beta * e_in)], axis=1).astype(MXU_DTYPE)
        a_raw = (beta * kq[:CHUNK] * decay).astype(MXU_DTYPE)
        return (d, n, rows, a_raw, rhs, attn, kdt, qc * e_in)

    def b_step(i, states):
        ns = (i, nc - 1 - i)
        rs = [_mm(nmq_ref[prv, d, ns[d]], states[d]) for d in range(2)]
        for d in range(2):
            o_ref[prv, ns[d] * CHUNK:(ns[d] + 1) * CHUNK, :] += rs[d][CHUNK:]
        return [states[d] * cd_ref[prv, d, ns[d]][0:1, :] + rs[d][:CHUNK] + c_ref[prv, d, ns[d]]
                for d in range(2)]

    zero = jnp.zeros((DN_HEAD_DIM, DN_HEAD_DIM), F32)
    box = {"states": [zero, zero], "b_next": 0, "minor": 0}
    lru = _lru_steps(lx_ref, lg_ref, wg_ref, bg_ref, lam_ref, y_ref, h_ref, lvl)
    inst = [prep(n, rows, kq, d) for (n, rows, kq) in kq_stage(0) for d in range(2)]
    for g in range(groups):
        pending = ([(n, rows, kq, d) for (n, rows, kq) in kq_stage(g + 1) for d in range(2)]
                   if g + 1 < groups else [])
        nxt = []
        b_end = (g + 1) * PHASE_A_GROUP

        def tick():
            if box["b_next"] < b_end:
                box["states"] = b_step(box["b_next"], box["states"])
                box["b_next"] += 1
            for _ in range(2):
                if pending:
                    nxt.append(prep(*pending.pop(0)))

        def minor():
            box["minor"] += 1
            if box["minor"] % LRU_PIECE_EVERY == 0:
                next(lru, None)

        idx = range(len(inst))
        n_inv = _unit_tri_inverse_minus_eye([v[3] for v in inst], lambda i, m: mask_ref[inst[i][0], m],
                                            tick, minor)
        while pending:
            nxt.append(prep(*pending.pop(0)))
        assert box["b_next"] == b_end
        uw, kuw, auw = [], [], []
        for i in idx:
            uw.append((inst[i][4].astype(F32) + _mm(n_inv[i], inst[i][4])).astype(MXU_DTYPE))
            minor()
        for i in idx:
            kuw.append(_mm(inst[i][6], uw[i]))
            auw.append(_mm(inst[i][5], uw[i]))
            minor()
        for i in idx:
            d, n, rows = inst[i][:3]
            c_ref[cur, d, n] = kuw[i][:, :DN_HEAD_DIM]
            nmq_ref[cur, d, n, :CHUNK, :] = (-kuw[i][:, DN_HEAD_DIM:]).astype(nmq_ref.dtype)
            nmq_ref[cur, d, n, CHUNK:, :] = (inst[i][7] - auw[i][:, DN_HEAD_DIM:]).astype(nmq_ref.dtype)
            o_ref[cur, rows, :] += auw[i][:, :DN_HEAD_DIM]
        inst = nxt
    for _ in lru:
        pass

    hz = 0.5 * z_ref[0].astype(F32)
    out_ref[0] = (_rms(o_ref[prv], ng_ref[...]) * (hz + hz * jnp.tanh(hz))).astype(out_ref.dtype)


def _scan_level_sizes(seq):
    sizes = [seq]
    while sizes[-1] > SCAN_BOTTOM_ROWS:
        assert sizes[-1] % (SCAN_FANOUT * SUBLANES) == 0
        sizes.append(sizes[-1] // SCAN_FANOUT)
    return sizes


def _lru_steps(lx_ref, lg_ref, wg_ref, bg_ref, lam_ref, y_ref, h_ref, lvl):
    nlev = (len(lvl) + 1) // 2
    ab = lvl[:nlev]
    cs = (None,) + tuple(lvl[nlev:])
    seq = lx_ref.shape[1]
    fan = SCAN_FANOUT
    sizes = [r.shape[2] for r in ab]

    for r0 in range(0, seq, LRU_PIECE_ROWS):
        rows = slice(r0, r0 + LRU_PIECE_ROWS)
        xc = lx_ref[0, rows, :].astype(F32)
        th = jnp.tanh(_mm(xc, wg_ref[0]) + bg_ref[0])
        half_x = 0.5 * xc
        for d in range(2):
            coef = (-0.5 * LRU_C) * jax.nn.softplus(-lam_ref[0, d:d + 1, :])
            log_a = coef * th[:, (2 * d) * LANES:(2 * d + 1) * LANES] + coef
            a = jnp.exp(log_a)
            ab[0][d, 0, rows, :] = a
            v = -jnp.tanh(log_a) * (a * a + 1.0)
            gated_x = half_x * th[:, (2 * d + 1) * LANES:(2 * d + 2) * LANES] + half_x
            ab[0][d, 1, rows, :] = jnp.where(v > 0.0, v * lax.rsqrt(v), 0.0) * gated_x
        yield
    for d in range(2):
        for l in range(1, nlev):
            cs[l][d, 0:SUBLANES, :] = jnp.zeros((SUBLANES, LANES), F32)
            cs[l][d, SUBLANES + sizes[l]:, :] = jnp.zeros((SUBLANES, LANES), F32)

    def slab(j, r0, n):
        return pl.ds(j + fan * r0, n, stride=fan)

    for l in range(nlev - 1):
        per_slab = sizes[l] // fan
        n = min(per_slab, LRU_PIECE_ROWS)
        for d in range(2):
            order = list(range(fan)) if d == 0 else list(range(fan - 1, -1, -1))
            for r0 in range(0, per_slab, n):
                p = ab[l][d, 0, slab(order[0], r0, n), :]
                h = ab[l][d, 1, slab(order[0], r0, n), :]
                for j in order[1:]:
                    aj = ab[l][d, 0, slab(j, r0, n), :]
                    h = aj * h + ab[l][d, 1, slab(j, r0, n), :]
                    p = aj * p
                    ab[l][d, 0, slab(j, r0, n), :] = p
                    ab[l][d, 1, slab(j, r0, n), :] = h
                ab[l + 1][d, 0, r0:r0 + n, :] = p
                ab[l + 1][d, 1, r0:r0 + n, :] = h
                yield

    top = nlev - 1
    rows = sizes[top]
    row = _row_iota((rows, LANES))
    for d in range(2):
        a = ab[top][d, 0]
        b = ab[top][d, 1]
        s = 1
        while s < rows:
            sh = rows - s if d == 1 else s
            keep = (row < rows - s) if d == 1 else (row >= s)
            b = jnp.where(keep, a * pltpu.roll(b, sh, 0) + b, b)
            a = jnp.where(keep, a * pltpu.roll(a, sh, 0), a)
            s *= 2
        if top == 0:
            ab[0][d, 1] = b
        else:
            cs[top][d, SUBLANES:SUBLANES + rows, :] = b
    yield

    for l in range(nlev - 2, -1, -1):
        per_slab = sizes[l] // fan
        n = min(per_slab, LRU_PIECE_ROWS)
        for r0 in range(0, per_slab, n):
            carry = [cs[l + 1][d, SUBLANES - 1 + 2 * d + r0:SUBLANES - 1 + 2 * d + r0 + n, :] for d in range(2)]
            for j in range(fan):
                true = [ab[l][d, 1, slab(j, r0, n), :] + ab[l][d, 0, slab(j, r0, n), :] * carry[d]
                        for d in range(2)]
                if l == 0:
                    h_ref[slab(j, r0, n), :] = true[0] + true[1]
                else:
                    for d in range(2):
                        cs[l][d, pl.ds(SUBLANES + j + fan * r0, n, stride=fan), :] = true[d]
            yield
    if nlev == 1:
        h_ref[...] = ab[0][0, 1] + ab[0][1, 1]
    for r0 in range(0, seq, LRU_PIECE_ROWS):
        rows = slice(r0, r0 + LRU_PIECE_ROWS)
        y_ref[0, rows, :] = (jax.nn.gelu(lg_ref[0, rows, :].astype(F32)) * h_ref[rows, :]).astype(y_ref.dtype)
        yield


def _seqmix(proj, pack, packt, norm_g, wgate, bgate, lam, batch, seq):
    nc = seq // CHUNK
    assert nc % PHASE_A_GROUP == 0
    blk = (1, seq, LANES)
    masks = jnp.asarray(_tri_masks(), MXU_DTYPE)
    items = batch * DN_HEADS
    nblk = LRU_WIDTH // LANES
    assert nblk == DN_HEADS
    j_lx = 4 * DN_HEADS
    j_lg = j_lx + nblk
    sizes = _scan_level_sizes(seq)

    def cur_item(t):
        i = jnp.minimum(t, items - 1)
        return i % DN_HEADS, i // DN_HEADS

    def prev_item(t):
        i = jnp.maximum(t - 1, 0)
        return i % DN_HEADS, i // DN_HEADS

    def col(j0):
        return pl.BlockSpec(blk, lambda t: (j0 + cur_item(t)[0], cur_item(t)[1], 0))

    def per_block(shape):
        return pl.BlockSpec(shape, lambda t: (cur_item(t)[0],) + (0,) * (len(shape) - 1))

    return pl.pallas_call(
        _seqmix_kernel,
        name="seqmix",
        grid=(items + 1,),
        in_specs=[col(0), col(DN_HEADS), col(2 * DN_HEADS),
                  pl.BlockSpec(blk, lambda t: (cur_item(t)[0], cur_item(t)[1], 0)),
                  pl.BlockSpec((1, PACKT_ROWS, seq), lambda t: (cur_item(t)[0], 0, cur_item(t)[1])),
                  pl.BlockSpec(blk, lambda t: (3 * DN_HEADS + prev_item(t)[0], prev_item(t)[1], 0)),
                  pl.BlockSpec((1, LANES), lambda t: (0, 0)),
                  pl.BlockSpec(masks.shape, lambda t: (0, 0, 0, 0)),
                  col(j_lx), col(j_lg),
                  per_block((1, LANES, 4 * LANES)), per_block((1, 1, 4 * LANES)), per_block((1, 2, LANES))],
        out_specs=[pl.BlockSpec(blk, lambda t: (prev_item(t)[0], prev_item(t)[1], 0)),
                   pl.BlockSpec(blk, lambda t: (cur_item(t)[0], cur_item(t)[1], 0))],
        out_shape=[jax.ShapeDtypeStruct((DN_HEADS, batch * seq, LANES), MXU_DTYPE),
                   jax.ShapeDtypeStruct((nblk, batch * seq, LANES), MXU_DTYPE)],
        scratch_shapes=(
            [pltpu.VMEM((2, 2, nc, DN_HEAD_DIM, DN_HEAD_DIM), F32),
             pltpu.VMEM((2, 2, nc, 2 * CHUNK, LANES), MXU_DTYPE),
             pltpu.VMEM((2, 2, nc, SUBLANES, LANES), F32),
             pltpu.VMEM((2, seq, LANES), F32),
             pltpu.VMEM((seq, LANES), F32)]
            + [pltpu.VMEM((2, 2, m, LANES), F32) for m in sizes]
            + [pltpu.VMEM((2, m + 2 * SUBLANES, LANES), F32) for m in sizes[1:]]),
        compiler_params=pltpu.CompilerParams(dimension_semantics=("arbitrary",),
                                             vmem_limit_bytes=VMEM_LIMIT),
    )(proj, proj, proj, pack, packt, proj, norm_g, masks, proj, proj, wgate, bgate, lam)


def _mixer_kernel(rp_ref, rm_ref, rn_ref, dnp_ref, dnm_ref, dnn_ref, lyp_ref, lym_ref, lyn_ref, p_ref,
                  lng_ref, wo_ref, g2_ref, wg_ref, wu_ref, cw_ref, cb_ref, wd_ref,
                  png_ref, pwg_ref, pbg_ref, pwp_ref, fg_ref, out_ref, mix_ref, hcat_ref, *, final):
    t = pl.program_id(1)
    tm = rm_ref.shape[1]
    halo = BF16_ROWS
    segments = ((0, halo, rp_ref, dnp_ref, lyp_ref), (halo, tm, rm_ref, dnm_ref, lym_ref),
                (halo + tm, halo, rn_ref, dnn_ref, lyn_ref))
    for lo, n, _, dn_ref, ly_ref in segments:
        for j in range(DN_HEADS):
            mix_ref[lo:lo + n, j * LANES:(j + 1) * LANES] = dn_ref[j]
        ly = jnp.concatenate([ly_ref[j] for j in range(ly_ref.shape[0])], axis=1).astype(F32)
        mix_ref[lo:lo + n, DN_WIDTH:] = _rms(ly, lng_ref[...]).astype(mix_ref.dtype)
    proj = jnp.dot(mix_ref[...], wo_ref[...], preferred_element_type=F32)
    r1 = rm_ref[0] + proj[halo:halo + tm]
    hcat_ref[halo:halo + tm, :] = _rms(r1, g2_ref[...]).astype(hcat_ref.dtype)
    h_prev = _rms(rp_ref[0] + proj[:halo], g2_ref[...])
    h_next = _rms(rn_ref[0] + proj[halo + tm:], g2_ref[...])
    hcat_ref[0:halo, :] = jnp.where(t > 0, h_prev, 0.0).astype(hcat_ref.dtype)
    hcat_ref[halo + tm:, :] = jnp.where(t < pl.num_programs(1) - 1, h_next, 0.0).astype(hcat_ref.dtype)

    rows = tm + 2 * halo
    acc = r1
    for c, width in enumerate(FF_CHUNKS):
        cs = slice(sum(FF_CHUNKS[:c]), sum(FF_CHUNKS[:c]) + width)
        gfull = jnp.dot(hcat_ref[...], wg_ref[:, cs], preferred_element_type=F32)
        gate = (pltpu.roll(gfull, 1, 0)[halo:halo + tm] * cw_ref[0:1, cs]
                + gfull[halo:halo + tm] * cw_ref[1:2, cs]
                + pltpu.roll(gfull, rows - 1, 0)[halo:halo + tm] * cw_ref[2:3, cs]
                + cb_ref[:, cs])
        up = jnp.dot(hcat_ref[halo:halo + tm, :], wu_ref[:, cs], preferred_element_type=F32)
        acc = acc + _mm(jax.nn.gelu(gate) * up, wd_ref[cs, :])
    pgate = _sigmoid(_mm(_rms(acc, png_ref[...]), pwg_ref[...]) + pbg_ref[...])
    res = acc + pgate * _mm(p_ref[0, 0], pwp_ref[...])
    out_ref[0] = _rms(res, fg_ref[...]) if final else res


def _mixer(r, dn, ly, p, layer, final, lng, wo, g2, wg, wu, cw, cb, wd, png, pwg, pbg, pwp, fg):
    batch, seq, _ = r.shape
    tm = TOKEN_TILE
    nt = seq // tm
    hb = tm // BF16_ROWS
    nhb = seq // BF16_ROWS
    nb = LRU_WIDTH // LANES

    def prev_blk(t):
        return jnp.maximum(t * hb - 1, 0)

    def next_blk(t):
        return jnp.minimum((t + 1) * hb, nhb - 1)

    def whole(shape):
        return pl.BlockSpec(shape, lambda b, t: (0,) * len(shape))

    def tok3(nlead):
        return [pl.BlockSpec((nlead, BF16_ROWS, LANES), lambda b, t: (0, b * nhb + prev_blk(t), 0)),
                pl.BlockSpec((nlead, tm, LANES), lambda b, t: (0, b * nt + t, 0)),
                pl.BlockSpec((nlead, BF16_ROWS, LANES), lambda b, t: (0, b * nhb + next_blk(t), 0))]

    return pl.pallas_call(
        functools.partial(_mixer_kernel, final=final),
        name="mixer",
        grid=(batch, nt),
        in_specs=[
            pl.BlockSpec((1, BF16_ROWS, D_MODEL), lambda b, t: (b, prev_blk(t), 0)),
            pl.BlockSpec((1, tm, D_MODEL), lambda b, t: (b, t, 0)),
            pl.BlockSpec((1, BF16_ROWS, D_MODEL), lambda b, t: (b, next_blk(t), 0)),
            *tok3(DN_HEADS), *tok3(nb),
            pl.BlockSpec((1, 1, tm, PLE_DIM), lambda b, t: (layer, b, t, 0)),
            whole((1, LRU_WIDTH)), whole((D_MODEL, D_MODEL)), whole((1, D_MODEL)),
            whole((D_MODEL, D_FF)), whole((D_MODEL, D_FF)), whole((3, D_FF)), whole((1, D_FF)),
            whole((D_FF, D_MODEL)), whole((1, D_MODEL)), whole((D_MODEL, D_MODEL)),
            whole((1, D_MODEL)), whole((PLE_DIM, D_MODEL)), whole((1, D_MODEL)),
        ],
        out_specs=pl.BlockSpec((1, tm, D_MODEL), lambda b, t: (b, t, 0)),
        out_shape=jax.ShapeDtypeStruct((batch, seq, D_MODEL), F32),
        scratch_shapes=[pltpu.VMEM((tm + 2 * BF16_ROWS, D_MODEL), MXU_DTYPE),
                        pltpu.VMEM((tm + 2 * BF16_ROWS, D_MODEL), MXU_DTYPE)],
        compiler_params=pltpu.CompilerParams(dimension_semantics=("arbitrary", "arbitrary"),
                                             vmem_limit_bytes=VMEM_LIMIT),
    )(r, r, r, dn, dn, dn, ly, ly, ly, p, lng, wo, g2, wg, wu, cw, cb, wd, png, pwg, pbg, pwp, fg)


def _row(v):
    return v.reshape(1, -1).astype(F32)


def _gate_lane_row(v):
    return jnp.zeros((1, LANES), F32).at[0, 8:16].set(v.reshape(-1).astype(F32))


def _lru_gate_weights(wa, ba, wx, bx):
    nblk = LRU_WIDTH // LANES
    per = LANES // LRU_BLOCK

    def bd(w):
        w = w.reshape(nblk, per, LRU_BLOCK, LRU_BLOCK)
        eye = jnp.eye(per, dtype=w.dtype)
        return jnp.einsum('npcd,pq->npcqd', w, eye).reshape(nblk, LANES, LANES)

    wcat = jnp.concatenate([bd(wa[0]), bd(wx[0]), bd(wa[1]), bd(wx[1])], axis=2)
    bcat = jnp.concatenate([ba[0].reshape(nblk, 1, LANES), bx[0].reshape(nblk, 1, LANES),
                            ba[1].reshape(nblk, 1, LANES), bx[1].reshape(nblk, 1, LANES)], axis=2)
    return (0.5 * wcat).astype(MXU_DTYPE), (0.5 * bcat).astype(F32)


def kernel(x, p, norm1_g, w_in, dn_conv_w, dn_a_log, dn_dt_bias, dn_norm_g, lru_conv_w, lru_conv_b,
           lru_wa, lru_ba, lru_wx, lru_bx, lru_lambda, lru_norm_g, w_out, norm2_g, ffn_wg, ffn_wu,
           ffn_conv_w, ffn_conv_b, ffn_wd, ple_norm_g, ple_wg, ple_bg, ple_wp, final_g):
    batch, seq, _ = x.shape
    depth = w_in.shape[0]
    assert seq % INPROJ_TILE == 0 and seq % CHUNK == 0
    nblk = LRU_WIDTH // LANES
    r = x
    for i in range(depth):
        w_perm = jnp.concatenate(
            [w_in[i][:, :BETA_OFF], w_in[i][:, LX_OFF:], w_in[i][:, BETA_OFF:LX_OFF],
             jnp.zeros((D_MODEL, LANES - (LX_OFF - BETA_OFF)), w_in.dtype)], axis=1).astype(MXU_DTYPE)
        proj, pack, packt = _inproj(r.reshape(batch * seq, D_MODEL), seq, _row(norm1_g[i]), w_perm,
                                    _gate_lane_row(dn_a_log[i]), _gate_lane_row(dn_dt_bias[i]),
                                    dn_conv_w[i], lru_conv_w[i], _row(lru_conv_b[i]))
        wgate, bgate = _lru_gate_weights(lru_wa[i], lru_ba[i], lru_wx[i], lru_bx[i])
        lam = jnp.transpose(lru_lambda[i].reshape(2, nblk, LANES), (1, 0, 2)).astype(F32)
        dn, ly = _seqmix(proj, pack, packt, _row(dn_norm_g[i]), wgate, bgate, lam, batch, seq)
        r = _mixer(r, dn, ly, p, i, i == depth - 1, _row(lru_norm_g[i]), w_out[i].astype(MXU_DTYPE),
                   _row(norm2_g[i]), ffn_wg[i].astype(MXU_DTYPE), ffn_wu[i].astype(MXU_DTYPE),
                   ffn_conv_w[i], _row(ffn_conv_b[i]), ffn_wd[i].astype(MXU_DTYPE), _row(ple_norm_g[i]),
                   ple_wg[i].astype(MXU_DTYPE), _row(ple_bg[i]), ple_wp[i].astype(MXU_DTYPE), _row(final_g))
    return r
```

```python
import functools

import jax
import numpy as np
import jax.numpy as jnp
from jax import lax
from jax.experimental import pallas as pl
from jax.experimental.pallas import tpu as pltpu

D_MODEL = 1024
DN_HEADS = 4
DN_HEAD_DIM = 128
DN_WIDTH = DN_HEADS * DN_HEAD_DIM
LRU_WIDTH = 512
LRU_BLOCKS = 8
LRU_BLOCK = LRU_WIDTH // LRU_BLOCKS
LRU_C = 8.0
D_FF = 2816
PLE_DIM = 256
EPS = 1e-6
Z_OFF = 3 * DN_WIDTH
BETA_OFF = 4 * DN_WIDTH
LX_OFF = BETA_OFF + 4 * DN_HEADS
IN_COLS = LX_OFF + 2 * LRU_WIDTH

LANES = 128
SUBLANES = 8
BF16_ROWS = 16
CHUNK = 128
PACKT_ROWS = 8
SCAN_FANOUT = 4
SCAN_BOTTOM_ROWS = 32
LRU_PIECE_ROWS = 64
LRU_PIECE_EVERY = 4
PHASE_A_GROUP = 8
MAIN_COLS = 4 * DN_WIDTH + 2 * LRU_WIDTH
N_MAIN_BLK = MAIN_COLS // LANES
PROJ_COLS = MAIN_COLS + LANES
TOKEN_TILE = 512
INPROJ_TILE = 2 * TOKEN_TILE
MXU_TILE = 256
FF_CHUNKS = (6 * MXU_TILE, 5 * MXU_TILE)
assert sum(FF_CHUNKS) == D_FF
VMEM_LIMIT = 56 * 1024 * 1024

MXU_DTYPE = jnp.bfloat16
F32 = jnp.float32


def _mm(a, b):
    return jnp.dot(a.astype(MXU_DTYPE), b.astype(MXU_DTYPE), preferred_element_type=F32)


def _mm_nt(a, b):
    return lax.dot_general(a.astype(MXU_DTYPE), b.astype(MXU_DTYPE), (((1,), (1,)), ((), ())),
                           preferred_element_type=F32)


def _rms(x, g):
    return x * lax.rsqrt(jnp.mean(x * x, axis=-1, keepdims=True) + EPS) * g


def _row_iota(shape):
    return lax.broadcasted_iota(jnp.int32, shape, 0)


def _col_iota(shape):
    return lax.broadcasted_iota(jnp.int32, shape, 1)


def _sigmoid(x):
    return 0.5 * jnp.tanh(0.5 * x) + 0.5


def _inproj_kernel(rb_ref, r_ref, ra_ref, g_ref, w_ref, alog_ref, dtb_ref, dcw_ref, lcw_ref, lcb_ref,
                   proj_ref, pack_ref, packt_ref, stage_ref, *, tiles_per_seq):
    tile_pos = pl.program_id(0) % tiles_per_seq
    tm = r_ref.shape[0]
    halo = SUBLANES
    sq = (CHUNK, CHUNK)
    tri = jnp.concatenate([(_col_iota(sq) <= _row_iota(sq)).astype(MXU_DTYPE),
                           (_col_iota(sq) >= _row_iota(sq)).astype(MXU_DTYPE)], axis=0)
    lane = _col_iota((CHUNK, LANES))

    def conv(res, j, w4):
        stage_ref[j] = res[:, j * LANES:(j + 1) * LANES]
        out = None
        for tap in range(w4.shape[0]):
            lo = halo - 2 + tap
            term = stage_ref[j, lo:lo + TOKEN_TILE, :] * w4[tap:tap + 1, :]
            out = term if out is None else out + term
        return out

    def half_silu(hy):
        return hy + hy * jnp.tanh(hy)

    def l2n(x):
        return x * lax.rsqrt(jnp.sum(x * x, axis=-1, keepdims=True) + EPS)

    for s0 in range(0, tm, TOKEN_TILE):
        if s0 == 0:
            before = jnp.where(tile_pos > 0, _rms(rb_ref[...], g_ref[...]), 0.0)
        else:
            before = _rms(r_ref[s0 - halo:s0, :], g_ref[...])
        if s0 + TOKEN_TILE == tm:
            after = jnp.where(tile_pos < tiles_per_seq - 1, _rms(ra_ref[...], g_ref[...]), 0.0)
        else:
            after = _rms(r_ref[s0 + TOKEN_TILE:s0 + TOKEN_TILE + halo, :], g_ref[...])
        h = jnp.concatenate([before, _rms(r_ref[s0:s0 + TOKEN_TILE, :], g_ref[...]), after], axis=0)
        res = _mm(h, w_ref[...])
        rows = slice(s0, s0 + TOKEN_TILE)
        for j in range(N_MAIN_BLK):
            cols = slice(j * LANES, (j + 1) * LANES)
            if j < 3 * DN_HEADS:
                y = half_silu(conv(res, j, 0.5 * dcw_ref[:, cols]))
                if j < DN_HEADS:
                    y = l2n(y) * (DN_HEAD_DIM ** -0.5)
                elif j < 2 * DN_HEADS:
                    y = l2n(y)
            elif 4 * DN_HEADS <= j < 4 * DN_HEADS + LRU_WIDTH // LANES:
                lc = slice((j - 4 * DN_HEADS) * LANES, (j - 4 * DN_HEADS + 1) * LANES)
                y = conv(res, j, lcw_ref[:, lc]) + lcb_ref[:, lc]
            else:
                y = res[halo:halo + TOKEN_TILE, cols]
            proj_ref[j, rows, :] = y.astype(proj_ref.dtype)
        gates = res[halo:halo + TOKEN_TILE, MAIN_COLS:]
        beta = _sigmoid(gates)
        g = -jnp.exp(alog_ref[...]) * jax.nn.softplus(gates + dtb_ref[...])
        for c in range(TOKEN_TILE // CHUNK):
            crow = slice(s0 + c * CHUNK, s0 + (c + 1) * CHUNK)
            gc = g[c * CHUNK:(c + 1) * CHUNK]
            p1 = gc.astype(MXU_DTYPE)
            r1 = gc - p1.astype(F32)
            p2 = r1.astype(MXU_DTYPE)
            p3 = (r1 - p2.astype(F32)).astype(MXU_DTYPE)
            cs = jnp.dot(tri, jnp.concatenate([p1, p2, p3], axis=1), preferred_element_type=F32)
            cs = cs[:, :LANES] + cs[:, LANES:2 * LANES] + cs[:, 2 * LANES:]
            prefix, suffix = cs[:CHUNK], cs[CHUNK:]
            pk = jnp.where(lane < 8, beta[c * CHUNK:(c + 1) * CHUNK],
                           jnp.where(lane < 12, prefix, suffix))
            for hd in range(DN_HEADS):
                pack_ref[hd, crow, :] = pk if hd == 0 else pltpu.roll(pk, LANES - hd, 1)
            pkt = pk.T
            for hd in range(DN_HEADS):
                packt_ref[hd, :, crow] = jnp.concatenate(
                    [pkt[8 + hd:9 + hd], pkt[12 + hd:13 + hd], jnp.zeros((PACKT_ROWS - 2, CHUNK), F32)], axis=0)


def _inproj(r2d, seq, g1, w, alog, dtb, dn_conv_w, lru_conv_w, lru_conv_b):
    t = r2d.shape[0]
    tm = INPROJ_TILE
    hb = tm // SUBLANES
    nhb = t // SUBLANES

    def whole(shape):
        return pl.BlockSpec(shape, lambda i: (0,) * len(shape))

    return pl.pallas_call(
        functools.partial(_inproj_kernel, tiles_per_seq=seq // tm),
        name="inproj",
        grid=(t // tm,),
        in_specs=[
            pl.BlockSpec((SUBLANES, D_MODEL), lambda i: (jnp.maximum(i * hb - 1, 0), 0)),
            pl.BlockSpec((tm, D_MODEL), lambda i: (i, 0)),
            pl.BlockSpec((SUBLANES, D_MODEL), lambda i: (jnp.minimum((i + 1) * hb, nhb - 1), 0)),
            whole((1, D_MODEL)), whole((D_MODEL, PROJ_COLS)), whole((1, LANES)), whole((1, LANES)),
            whole(dn_conv_w.shape), whole(lru_conv_w.shape), whole(lru_conv_b.shape),
        ],
        out_specs=[
            pl.BlockSpec((N_MAIN_BLK, tm, LANES), lambda i: (0, i, 0)),
            pl.BlockSpec((DN_HEADS, tm, LANES), lambda i: (0, i, 0)),
            pl.BlockSpec((DN_HEADS, PACKT_ROWS, tm), lambda i: (0, 0, i)),
        ],
        out_shape=[
            jax.ShapeDtypeStruct((N_MAIN_BLK, t, LANES), MXU_DTYPE),
            jax.ShapeDtypeStruct((DN_HEADS, t, LANES), F32),
            jax.ShapeDtypeStruct((DN_HEADS, PACKT_ROWS, t), F32),
        ],
        scratch_shapes=[pltpu.VMEM((N_MAIN_BLK, TOKEN_TILE + 2 * SUBLANES, LANES), F32)],
        compiler_params=pltpu.CompilerParams(dimension_semantics=("arbitrary",),
                                             vmem_limit_bytes=VMEM_LIMIT),
    )(r2d, r2d, r2d, g1, w, alog, dtb, dn_conv_w, lru_conv_w, lru_conv_b)


N_TRI_MASKS = 6


def _tri_masks():
    i = np.arange(CHUNK)[:, None]
    j = np.arange(CHUNK)[None, :]
    out = np.zeros((2, N_TRI_MASKS, CHUNK, CHUNK), np.float32)
    for d in range(2):
        hi, lo = (j, i) if d == 1 else (i, j)
        out[d, 0] = (hi // SUBLANES == lo // SUBLANES) & (hi > lo)
        s, lvl = SUBLANES, 1
        while s < CHUNK:
            out[d, lvl] = (hi // (2 * s) == lo // (2 * s)) & ((hi // s) % 2 == 1) & ((lo // s) % 2 == 0)
            s, lvl = 2 * s, lvl + 1
        out[d, N_TRI_MASKS - 1] = hi >= lo
    return out


def _unit_tri_inverse_minus_eye(a_list, mask_of, tick, minor):
    idx = range(len(a_list))

    def stage(fn):
        out = []
        for i in idx:
            out.append(fn(i))
            minor()
        tick()
        return out

    x = [a_list[i] * mask_of(i, 0) for i in idx]
    y2 = stage(lambda i: _mm(x[i], x[i]))
    y4 = stage(lambda i: _mm(y2[i], y2[i]))
    xy2 = stage(lambda i: _mm(x[i], y2[i]))
    m1 = [y2[i] - x[i].astype(F32) - xy2[i] for i in idx]
    m1y4 = stage(lambda i: _mm(m1[i], y4[i]))
    n = [(m1[i] + y4[i] + m1y4[i]).astype(MXU_DTYPE) for i in idx]
    for lvl in range(1, N_TRI_MASKS - 1):
        b = [a_list[i] * mask_of(i, lvl) for i in idx]
        p = stage(lambda i: b[i].astype(F32) + _mm(n[i], b[i]))
        pn = stage(lambda i: _mm(p[i], n[i]))
        n = [(n[i].astype(F32) - p[i] - pn[i]).astype(MXU_DTYPE) for i in idx]
    return n


def _seqmix_kernel(q_ref, k_ref, v_ref, pack_ref, packt_ref, z_ref, ng_ref, mask_ref,
                   lx_ref, lg_ref, wg_ref, bg_ref, lam_ref, out_ref, y_ref,
                   c_ref, nmq_ref, cd_ref, o_ref, h_ref, *lvl):
    t = pl.program_id(0)
    cur = t % 2
    prv = 1 - cur
    seq = q_ref.shape[1]
    nc = seq // CHUNK

    @pl.when(t == 0)
    def _():
        c_ref[1] = jnp.zeros(c_ref.shape[1:], c_ref.dtype)
        nmq_ref[1] = jnp.zeros(nmq_ref.shape[1:], nmq_ref.dtype)
        cd_ref[1] = jnp.zeros(cd_ref.shape[1:], cd_ref.dtype)
        o_ref[1] = jnp.zeros(o_ref.shape[1:], o_ref.dtype)

    o_ref[cur] = jnp.zeros(o_ref.shape[1:], o_ref.dtype)

    sq = (CHUNK, CHUNK)
    eye = (_row_iota(sq) == _col_iota(sq)).astype(MXU_DTYPE)

    groups = nc // PHASE_A_GROUP

    def kq_stage(g):
        out = []
        for n in range(g * PHASE_A_GROUP, (g + 1) * PHASE_A_GROUP):
            rows = slice(n * CHUNK, (n + 1) * CHUNK)
            qc, kc = q_ref[0, rows, :], k_ref[0, rows, :]
            out.append((n, rows, _mm_nt(jnp.concatenate([kc, qc, eye], axis=0), kc)))
        return out

    def prep(n, rows, kq, d):
        pk = pack_ref[0, rows, :]
        qc = q_ref[0, rows, :].astype(F32)
        kc = k_ref[0, rows, :].astype(F32)
        vc = v_ref[0, rows, :].astype(F32)
        beta = jnp.broadcast_to(pk[:, 4 * d:4 * d + 1], sq)
        gc = jnp.broadcast_to(pk[:, 8 + 4 * d:9 + 4 * d], sq)
        gc_row = packt_ref[0, d:d + 1, rows]
        glast = gc[0:1, :] if d == 1 else gc[CHUNK - 1:CHUNK, :]
        e_in = jnp.exp(gc)
        decay = jnp.exp(jnp.minimum(gc - gc_row, 0.0)) * mask_ref[d, N_TRI_MASKS - 1].astype(F32)
        attn = (kq[CHUNK:2 * CHUNK] * decay).astype(MXU_DTYPE)
        kdt = (kq[2 * CHUNK:] * jnp.exp(glast - gc_row)).astype(MXU_DTYPE)
        cd_ref[cur, d, n] = jnp.exp(jnp.broadcast_to(glast, (SUBLANES, LANES)))
        rhs = jnp.concatenate([vc * beta, kc * (beta * e_in)], axis=1).astype(MXU_DTYPE)
        a_raw = (beta * kq[:CHUNK] * decay).astype(MXU_DTYPE)
        return (d, n, rows, a_raw, rhs, attn, kdt, qc * e_in)

    def b_step(i, states):
        ns = (i, nc - 1 - i)
        rs = [_mm(nmq_ref[prv, d, ns[d]], states[d]) for d in range(2)]
        for d in range(2):
            o_ref[prv, ns[d] * CHUNK:(ns[d] + 1) * CHUNK, :] += rs[d][CHUNK:]
        return [states[d] * cd_ref[prv, d, ns[d]][0:1, :] + rs[d][:CHUNK] + c_ref[prv, d, ns[d]]
                for d in range(2)]

    zero = jnp.zeros((DN_HEAD_DIM, DN_HEAD_DIM), F32)
    box = {"states": [zero, zero], "b_next": 0, "minor": 0}
    lru = _lru_steps(lx_ref, lg_ref, wg_ref, bg_ref, lam_ref, y_ref, h_ref, lvl)
    inst = [prep(n, rows, kq, d) for (n, rows, kq) in kq_stage(0) for d in range(2)]
    for g in range(groups):
        pending = ([(n, rows, kq, d) for (n, rows, kq) in kq_stage(g + 1) for d in range(2)]
                   if g + 1 < groups else [])
        nxt = []
        b_end = (g + 1) * PHASE_A_GROUP

        def tick():
            if box["b_next"] < b_end:
                box["states"] = b_step(box["b_next"], box["states"])
                box["b_next"] += 1
            for _ in range(2):
                if pending:
                    nxt.append(prep(*pending.pop(0)))

        def minor():
            box["minor"] += 1
            if box["minor"] % LRU_PIECE_EVERY == 0:
                next(lru, None)

        idx = range(len(inst))
        n_inv = _unit_tri_inverse_minus_eye([v[3] for v in inst], lambda i, m: mask_ref[inst[i][0], m],
                                            tick, minor)
        while pending:
            nxt.append(prep(*pending.pop(0)))
        assert box["b_next"] == b_end
        uw = []
        for i in idx:
            uw.append((inst[i][4].astype(F32) + _mm(n_inv[i], inst[i][4])).astype(MXU_DTYPE))
            minor()
        for i in idx:
            d, n, rows = inst[i][:3]
            kuw = _mm(inst[i][6], uw[i])
            auw = _mm(inst[i][5], uw[i])
            minor()
            c_ref[cur, d, n] = kuw[:, :DN_HEAD_DIM]
            nmq_ref[cur, d, n, :CHUNK, :] = (-kuw[:, DN_HEAD_DIM:]).astype(nmq_ref.dtype)
            nmq_ref[cur, d, n, CHUNK:, :] = (inst[i][7] - auw[:, DN_HEAD_DIM:]).astype(nmq_ref.dtype)
            o_ref[cur, rows, :] += auw[:, :DN_HEAD_DIM]
        inst = nxt
    for _ in lru:
        pass

    hz = 0.5 * z_ref[0].astype(F32)
    out_ref[0] = (_rms(o_ref[prv], ng_ref[...]) * (hz + hz * jnp.tanh(hz))).astype(out_ref.dtype)


def _scan_level_sizes(seq):
    sizes = [seq]
    while sizes[-1] > SCAN_BOTTOM_ROWS:
        assert sizes[-1] % (SCAN_FANOUT * SUBLANES) == 0
        sizes.append(sizes[-1] // SCAN_FANOUT)
    return sizes


def _lru_steps(lx_ref, lg_ref, wg_ref, bg_ref, lam_ref, y_ref, h_ref, lvl):
    nlev = (len(lvl) + 1) // 2
    ab = lvl[:nlev]
    cs = (None,) + tuple(lvl[nlev:])
    seq = lx_ref.shape[1]
    fan = SCAN_FANOUT
    sizes = [r.shape[2] for r in ab]

    for r0 in range(0, seq, LRU_PIECE_ROWS):
        rows = slice(r0, r0 + LRU_PIECE_ROWS)
        xc = lx_ref[0, rows, :].astype(F32)
        th = jnp.tanh(_mm(xc, wg_ref[0]) + bg_ref[0])
        half_x = 0.5 * xc
        for d in range(2):
            coef = (-0.5 * LRU_C) * jax.nn.softplus(-lam_ref[0, d:d + 1, :])
            log_a = coef * th[:, (2 * d) * LANES:(2 * d + 1) * LANES] + coef
            a = jnp.exp(log_a)
            ab[0][d, 0, rows, :] = a
            v = -jnp.tanh(log_a) * (a * a + 1.0)
            gated_x = half_x * th[:, (2 * d + 1) * LANES:(2 * d + 2) * LANES] + half_x
            ab[0][d, 1, rows, :] = jnp.where(v > 0.0, v * lax.rsqrt(v), 0.0) * gated_x
        yield
    for d in range(2):
        for l in range(1, nlev):
            cs[l][d, 0:SUBLANES, :] = jnp.zeros((SUBLANES, LANES), F32)
            cs[l][d, SUBLANES + sizes[l]:, :] = jnp.zeros((SUBLANES, LANES), F32)

    def slab(j, r0, n):
        return pl.ds(j + fan * r0, n, stride=fan)

    for l in range(nlev - 1):
        per_slab = sizes[l] // fan
        n = min(per_slab, LRU_PIECE_ROWS)
        for d in range(2):
            order = list(range(fan)) if d == 0 else list(range(fan - 1, -1, -1))
            for r0 in range(0, per_slab, n):
                p = ab[l][d, 0, slab(order[0], r0, n), :]
                h = ab[l][d, 1, slab(order[0], r0, n), :]
                for j in order[1:]:
                    aj = ab[l][d, 0, slab(j, r0, n), :]
                    h = aj * h + ab[l][d, 1, slab(j, r0, n), :]
                    p = aj * p
                    ab[l][d, 0, slab(j, r0, n), :] = p
                    ab[l][d, 1, slab(j, r0, n), :] = h
                ab[l + 1][d, 0, r0:r0 + n, :] = p
                ab[l + 1][d, 1, r0:r0 + n, :] = h
                yield

    top = nlev - 1
    rows = sizes[top]
    row = _row_iota((rows, LANES))
    for d in range(2):
        a = ab[top][d, 0]
        b = ab[top][d, 1]
        s = 1
        while s < rows:
            sh = rows - s if d == 1 else s
            keep = (row < rows - s) if d == 1 else (row >= s)
            b = jnp.where(keep, a * pltpu.roll(b, sh, 0) + b, b)
            a = jnp.where(keep, a * pltpu.roll(a, sh, 0), a)
            s *= 2
        if top == 0:
            ab[0][d, 1] = b
        else:
            cs[top][d, SUBLANES:SUBLANES + rows, :] = b
    yield

    for l in range(nlev - 2, -1, -1):
        per_slab = sizes[l] // fan
        n = min(per_slab, LRU_PIECE_ROWS)
        for r0 in range(0, per_slab, n):
            carry = [cs[l + 1][d, SUBLANES - 1 + 2 * d + r0:SUBLANES - 1 + 2 * d + r0 + n, :] for d in range(2)]
            for j in range(fan):
                true = [ab[l][d, 1, slab(j, r0, n), :] + ab[l][d, 0, slab(j, r0, n), :] * carry[d]
                        for d in range(2)]
                if l == 0:
                    h_ref[slab(j, r0, n), :] = true[0] + true[1]
                else:
                    for d in range(2):
                        cs[l][d, pl.ds(SUBLANES + j + fan * r0, n, stride=fan), :] = true[d]
            yield
    if nlev == 1:
        h_ref[...] = ab[0][0, 1] + ab[0][1, 1]
    for r0 in range(0, seq, LRU_PIECE_ROWS):
        rows = slice(r0, r0 + LRU_PIECE_ROWS)
        y_ref[0, rows, :] = (jax.nn.gelu(lg_ref[0, rows, :].astype(F32)) * h_ref[rows, :]).astype(y_ref.dtype)
        yield


def _seqmix(proj, pack, packt, norm_g, wgate, bgate, lam, batch, seq):
    nc = seq // CHUNK
    assert nc % PHASE_A_GROUP == 0
    blk = (1, seq, LANES)
    masks = jnp.asarray(_tri_masks(), MXU_DTYPE)
    items = batch * DN_HEADS
    nblk = LRU_WIDTH // LANES
    assert nblk == DN_HEADS
    j_lx = 4 * DN_HEADS
    j_lg = j_lx + nblk
    sizes = _scan_level_sizes(seq)

    def cur_item(t):
        i = jnp.minimum(t, items - 1)
        return i % DN_HEADS, i // DN_HEADS

    def prev_item(t):
        i = jnp.maximum(t - 1, 0)
        return i % DN_HEADS, i // DN_HEADS

    def col(j0):
        return pl.BlockSpec(blk, lambda t: (j0 + cur_item(t)[0], cur_item(t)[1], 0))

    def per_block(shape):
        return pl.BlockSpec(shape, lambda t: (cur_item(t)[0],) + (0,) * (len(shape) - 1))

    return pl.pallas_call(
        _seqmix_kernel,
        name="seqmix",
        grid=(items + 1,),
        in_specs=[col(0), col(DN_HEADS), col(2 * DN_HEADS),
                  pl.BlockSpec(blk, lambda t: (cur_item(t)[0], cur_item(t)[1], 0)),
                  pl.BlockSpec((1, PACKT_ROWS, seq), lambda t: (cur_item(t)[0], 0, cur_item(t)[1])),
                  pl.BlockSpec(blk, lambda t: (3 * DN_HEADS + prev_item(t)[0], prev_item(t)[1], 0)),
                  pl.BlockSpec((1, LANES), lambda t: (0, 0)),
                  pl.BlockSpec(masks.shape, lambda t: (0, 0, 0, 0)),
                  col(j_lx), col(j_lg),
                  per_block((1, LANES, 4 * LANES)), per_block((1, 1, 4 * LANES)), per_block((1, 2, LANES))],
        out_specs=[pl.BlockSpec(blk, lambda t: (prev_item(t)[0], prev_item(t)[1], 0)),
                   pl.BlockSpec(blk, lambda t: (cur_item(t)[0], cur_item(t)[1], 0))],
        out_shape=[jax.ShapeDtypeStruct((DN_HEADS, batch * seq, LANES), MXU_DTYPE),
                   jax.ShapeDtypeStruct((nblk, batch * seq, LANES), MXU_DTYPE)],
        scratch_shapes=(
            [pltpu.VMEM((2, 2, nc, DN_HEAD_DIM, DN_HEAD_DIM), F32),
             pltpu.VMEM((2, 2, nc, 2 * CHUNK, LANES), MXU_DTYPE),
             pltpu.VMEM((2, 2, nc, SUBLANES, LANES), F32),
             pltpu.VMEM((2, seq, LANES), F32),
             pltpu.VMEM((seq, LANES), F32)]
            + [pltpu.VMEM((2, 2, m, LANES), F32) for m in sizes]
            + [pltpu.VMEM((2, m + 2 * SUBLANES, LANES), F32) for m in sizes[1:]]),
        compiler_params=pltpu.CompilerParams(dimension_semantics=("arbitrary",),
                                             vmem_limit_bytes=VMEM_LIMIT),
    )(proj, proj, proj, pack, packt, proj, norm_g, masks, proj, proj, wgate, bgate, lam)


def _mixer_kernel(rp_ref, rm_ref, rn_ref, dnp_ref, dnm_ref, dnn_ref, lyp_ref, lym_ref, lyn_ref, p_ref,
                  lng_ref, wo_ref, g2_ref, wg_ref, wu_ref, cw_ref, cb_ref, wd_ref,
                  png_ref, pwg_ref, pbg_ref, pwp_ref, fg_ref, out_ref, mix_ref, hcat_ref, *, final):
    t = pl.program_id(1)
    tm = rm_ref.shape[1]
    halo = BF16_ROWS
    segments = ((0, halo, rp_ref, dnp_ref, lyp_ref), (halo, tm, rm_ref, dnm_ref, lym_ref),
                (halo + tm, halo, rn_ref, dnn_ref, lyn_ref))
    for lo, n, _, dn_ref, ly_ref in segments:
        for j in range(DN_HEADS):
            mix_ref[lo:lo + n, j * LANES:(j + 1) * LANES] = dn_ref[j]
        ly = jnp.concatenate([ly_ref[j] for j in range(ly_ref.shape[0])], axis=1).astype(F32)
        mix_ref[lo:lo + n, DN_WIDTH:] = _rms(ly, lng_ref[...]).astype(mix_ref.dtype)
    proj = jnp.dot(mix_ref[...], wo_ref[...], preferred_element_type=F32)
    r1 = rm_ref[0] + proj[halo:halo + tm]
    hcat_ref[halo:halo + tm, :] = _rms(r1, g2_ref[...]).astype(hcat_ref.dtype)
    h_prev = _rms(rp_ref[0] + proj[:halo], g2_ref[...])
    h_next = _rms(rn_ref[0] + proj[halo + tm:], g2_ref[...])
    hcat_ref[0:halo, :] = jnp.where(t > 0, h_prev, 0.0).astype(hcat_ref.dtype)
    hcat_ref[halo + tm:, :] = jnp.where(t < pl.num_programs(1) - 1, h_next, 0.0).astype(hcat_ref.dtype)

    rows = tm + 2 * halo
    acc = r1
    for c, width in enumerate(FF_CHUNKS):
        cs = slice(sum(FF_CHUNKS[:c]), sum(FF_CHUNKS[:c]) + width)
        gfull = jnp.dot(hcat_ref[...], wg_ref[:, cs], preferred_element_type=F32)
        gate = (pltpu.roll(gfull, 1, 0)[halo:halo + tm] * cw_ref[0:1, cs]
                + gfull[halo:halo + tm] * cw_ref[1:2, cs]
                + pltpu.roll(gfull, rows - 1, 0)[halo:halo + tm] * cw_ref[2:3, cs]
                + cb_ref[:, cs])
        up = jnp.dot(hcat_ref[halo:halo + tm, :], wu_ref[:, cs], preferred_element_type=F32)
        acc = acc + _mm(jax.nn.gelu(gate) * up, wd_ref[cs, :])
    pgate = _sigmoid(_mm(_rms(acc, png_ref[...]), pwg_ref[...]) + pbg_ref[...])
    res = acc + pgate * _mm(p_ref[0, 0], pwp_ref[...])
    out_ref[0] = _rms(res, fg_ref[...]) if final else res


def _mixer(r, dn, ly, p, layer, final, lng, wo, g2, wg, wu, cw, cb, wd, png, pwg, pbg, pwp, fg):
    batch, seq, _ = r.shape
    tm = TOKEN_TILE
    nt = seq // tm
    hb = tm // BF16_ROWS
    nhb = seq // BF16_ROWS
    nb = LRU_WIDTH // LANES

    def prev_blk(t):
        return jnp.maximum(t * hb - 1, 0)

    def next_blk(t):
        return jnp.minimum((t + 1) * hb, nhb - 1)

    def whole(shape):
        return pl.BlockSpec(shape, lambda b, t: (0,) * len(shape))

    def tok3(nlead):
        return [pl.BlockSpec((nlead, BF16_ROWS, LANES), lambda b, t: (0, b * nhb + prev_blk(t), 0)),
                pl.BlockSpec((nlead, tm, LANES), lambda b, t: (0, b * nt + t, 0)),
                pl.BlockSpec((nlead, BF16_ROWS, LANES), lambda b, t: (0, b * nhb + next_blk(t), 0))]

    return pl.pallas_call(
        functools.partial(_mixer_kernel, final=final),
        name="mixer",
        grid=(batch, nt),
        in_specs=[
            pl.BlockSpec((1, BF16_ROWS, D_MODEL), lambda b, t: (b, prev_blk(t), 0)),
            pl.BlockSpec((1, tm, D_MODEL), lambda b, t: (b, t, 0)),
            pl.BlockSpec((1, BF16_ROWS, D_MODEL), lambda b, t: (b, next_blk(t), 0)),
            *tok3(DN_HEADS), *tok3(nb),
            pl.BlockSpec((1, 1, tm, PLE_DIM), lambda b, t: (layer, b, t, 0)),
            whole((1, LRU_WIDTH)), whole((D_MODEL, D_MODEL)), whole((1, D_MODEL)),
            whole((D_MODEL, D_FF)), whole((D_MODEL, D_FF)), whole((3, D_FF)), whole((1, D_FF)),
            whole((D_FF, D_MODEL)), whole((1, D_MODEL)), whole((D_MODEL, D_MODEL)),
            whole((1, D_MODEL)), whole((PLE_DIM, D_MODEL)), whole((1, D_MODEL)),
        ],
        out_specs=pl.BlockSpec((1, tm, D_MODEL), lambda b, t: (b, t, 0)),
        out_shape=jax.ShapeDtypeStruct((batch, seq, D_MODEL), F32),
        scratch_shapes=[pltpu.VMEM((tm + 2 * BF16_ROWS, D_MODEL), MXU_DTYPE),
                        pltpu.VMEM((tm + 2 * BF16_ROWS, D_MODEL), MXU_DTYPE)],
        compiler_params=pltpu.CompilerParams(dimension_semantics=("arbitrary", "arbitrary"),
                                             vmem_limit_bytes=VMEM_LIMIT),
    )(r, r, r, dn, dn, dn, ly, ly, ly, p, lng, wo, g2, wg, wu, cw, cb, wd, png, pwg, pbg, pwp, fg)


def _row(v):
    return v.reshape(1, -1).astype(F32)


def _gate_lane_row(v):
    return jnp.zeros((1, LANES), F32).at[0, 8:16].set(v.reshape(-1).astype(F32))


def _lru_gate_weights(wa, ba, wx, bx):
    nblk = LRU_WIDTH // LANES
    per = LANES // LRU_BLOCK

    def bd(w):
        w = w.reshape(nblk, per, LRU_BLOCK, LRU_BLOCK)
        eye = jnp.eye(per, dtype=w.dtype)
        return jnp.einsum('npcd,pq->npcqd', w, eye).reshape(nblk, LANES, LANES)

    wcat = jnp.concatenate([bd(wa[0]), bd(wx[0]), bd(wa[1]), bd(wx[1])], axis=2)
    bcat = jnp.concatenate([ba[0].reshape(nblk, 1, LANES), bx[0].reshape(nblk, 1, LANES),
                            ba[1].reshape(nblk, 1, LANES), bx[1].reshape(nblk, 1, LANES)], axis=2)
    return (0.5 * wcat).astype(MXU_DTYPE), (0.5 * bcat).astype(F32)


def kernel(x, p, norm1_g, w_in, dn_conv_w, dn_a_log, dn_dt_bias, dn_norm_g, lru_conv_w, lru_conv_b,
           lru_wa, lru_ba, lru_wx, lru_bx, lru_lambda, lru_norm_g, w_out, norm2_g, ffn_wg, ffn_wu,
           ffn_conv_w, ffn_conv_b, ffn_wd, ple_norm_g, ple_wg, ple_bg, ple_wp, final_g):
    batch, seq, _ = x.shape
    depth = w_in.shape[0]
    assert seq % INPROJ_TILE == 0 and seq % CHUNK == 0
    nblk = LRU_WIDTH // LANES
    r = x
    for i in range(depth):
        w_perm = jnp.concatenate(
            [w_in[i][:, :BETA_OFF], w_in[i][:, LX_OFF:], w_in[i][:, BETA_OFF:LX_OFF],
             jnp.zeros((D_MODEL, LANES - (LX_OFF - BETA_OFF)), w_in.dtype)], axis=1).astype(MXU_DTYPE)
        proj, pack, packt = _inproj(r.reshape(batch * seq, D_MODEL), seq, _row(norm1_g[i]), w_perm,
                                    _gate_lane_row(dn_a_log[i]), _gate_lane_row(dn_dt_bias[i]),
                                    dn_conv_w[i], lru_conv_w[i], _row(lru_conv_b[i]))
        wgate, bgate = _lru_gate_weights(lru_wa[i], lru_ba[i], lru_wx[i], lru_bx[i])
        lam = jnp.transpose(lru_lambda[i].reshape(2, nblk, LANES), (1, 0, 2)).astype(F32)
        dn, ly = _seqmix(proj, pack, packt, _row(dn_norm_g[i]), wgate, bgate, lam, batch, seq)
        r = _mixer(r, dn, ly, p, i, i == depth - 1, _row(lru_norm_g[i]), w_out[i].astype(MXU_DTYPE),
                   _row(norm2_g[i]), ffn_wg[i].astype(MXU_DTYPE), ffn_wu[i].astype(MXU_DTYPE),
                   ffn_conv_w[i], _row(ffn_conv_b[i]), ffn_wd[i].astype(MXU_DTYPE), _row(ple_norm_g[i]),
                   ple_wg[i].astype(MXU_DTYPE), _row(ple_bg[i]), ple_wp[i].astype(MXU_DTYPE), _row(final_g))
    return r
```
